```python
import math, functools
import jax, jax.numpy as jnp
from jax import lax
import numpy as np

D_MODEL = 2048
BATCH = 8
SEQ = 4096
DEPTH = 4

GRID_W = 64
CTX_LEN = 256
CHUNK = 128
VW = 2048
H_R = 8
R_DK = 128
R_DV = VW // H_R
H_M = 8
M_DK = 128
M_DV = VW // H_M
R_QK = H_R * R_DK
M_QK = H_M * M_DK
N_GATES = 4 * H_M
IN_SIZES = (R_QK, R_QK, VW, VW, M_QK, M_QK, VW, VW, N_GATES, 2 * D_MODEL)
IN_COLS = 2 * R_QK + 2 * VW + 2 * M_QK + 2 * VW + N_GATES + 2 * D_MODEL
D_FF = 5632
CONV_W = 3
N_MOD = 6
ROPE_BASE = 10000.0
EPS = 1e-6

kernel_name = 'hybrid_retention_mlstm_convffn_dit'


def rms_norm(x, g):
    xf = x.astype(jnp.float32)
    y = xf * lax.rsqrt(jnp.mean(xf * xf, axis=-1, keepdims=True) + EPS)
    return (y * g.astype(jnp.float32)).astype(x.dtype)


def head_rms(y, g):
    H, d = y.shape[1], y.shape[3]
    y = y * lax.rsqrt(jnp.mean(y * y, axis=-1, keepdims=True) + EPS)
    return y * g.astype(jnp.float32).reshape(H, 1, d)


def modulate(h, shift, scale):
    return h * (1 + scale) + shift


def dwconv(x, w, b):
    T = x.shape[1]
    half = w.shape[0] // 2
    xp = jnp.pad(x, ((0, 0), (half, half), (0, 0)))
    out = b
    for k in range(w.shape[0]):
        out = out + xp[:, k:k + T] * w[k]
    return out


def split_cols(p, sizes):
    offs, o = [], 0
    for s in sizes[:-1]:
        o += s
        offs.append(o)
    return jnp.split(p, offs, axis=-1)


def split_heads(x, h):
    B, T, _ = x.shape
    return x.reshape(B, T, h, -1).transpose(0, 2, 1, 3)


def merge_heads(x):
    B, H, T, d = x.shape
    return x.transpose(0, 2, 1, 3).reshape(B, T, H * d)


def rope_1d(x, pos):
    half = x.shape[-1] // 2
    freqs = ROPE_BASE ** (-jnp.arange(half, dtype=jnp.float32) / half)
    ang = pos[:, None] * freqs[None, :]
    cos, sin = jnp.cos(ang).astype(x.dtype), jnp.sin(ang).astype(x.dtype)
    x1, x2 = x[..., :half], x[..., half:]
    return jnp.concatenate([x1 * cos - x2 * sin, x1 * sin + x2 * cos], axis=-1)


def axial_rope(x, rows, cols):
    half = x.shape[-1] // 2
    return jnp.concatenate([rope_1d(x[..., :half], rows), rope_1d(x[..., half:], cols)], axis=-1)


def to_chunks(a):
    B, H, T = a.shape[:3]
    a = a.reshape((B, H, T // CHUNK, CHUNK) + a.shape[3:])
    return jnp.moveaxis(a, 2, 0)


def from_chunks(a):
    a = jnp.moveaxis(a, 0, 2)
    B, H, N, L = a.shape[:4]
    return a.reshape((B, H, N * L) + a.shape[4:])


def retention_scan(q, k, v, state0, log_gamma):
    q, k, v = (a.astype(jnp.float32) for a in (q, k, v))
    idx = jnp.arange(CHUNK, dtype=jnp.float32)
    lg = log_gamma.astype(jnp.float32)[:, None, None]
    rel = idx[:, None] - idx[None, :]
    intra = jnp.where(rel >= 0, jnp.exp(jnp.maximum(rel, 0.0) * lg), 0.0)
    q_dec = jnp.exp((idx + 1.0)[:, None] * lg)
    k_dec = jnp.exp((CHUNK - 1.0 - idx)[:, None] * lg)
    chunk_dec = jnp.exp(CHUNK * lg)

    def step(S, inp):
        qc, kc, vc = inp
        s = jnp.einsum('bhid,bhjd->bhij', qc, kc) * intra
        y = jnp.einsum('bhij,bhjv->bhiv', s, vc) + jnp.einsum('bhid,bhdv->bhiv', qc * q_dec, S)
        S = chunk_dec * S + jnp.einsum('bhjd,bhjv->bhdv', kc * k_dec, vc)
        return S, y

    S, ys = lax.scan(step, state0, (to_chunks(q), to_chunks(k), to_chunks(v)))
    return from_chunks(ys), S


def mlstm_scan(q, k, v, ig, lf, state0):
    q, k, v = (a.astype(jnp.float32) for a in (q, k, v))
    causal = jnp.tril(jnp.ones((CHUNK, CHUNK), dtype=bool))

    def step(carry, inp):
        C, n, m = carry
        qc, kc, vc, ic, fc = inp
        b = jnp.cumsum(fc, axis=-1)
        d = jnp.where(causal, b[..., :, None] - b[..., None, :] + ic[..., None, :], -jnp.inf)
        inter = b + m[..., None]
        m_t = jnp.maximum(d.max(axis=-1), inter)
        s = jnp.einsum('bhid,bhjd->bhij', qc, kc) * jnp.exp(d - m_t[..., None])
        a = jnp.exp(inter - m_t)
        num = jnp.einsum('bhij,bhjv->bhiv', s, vc) + a[..., None] * jnp.einsum('bhid,bhdv->bhiv', qc, C)
        den = s.sum(axis=-1) + a * jnp.einsum('bhid,bhd->bhi', qc, n)
        h = num / jnp.maximum(jnp.abs(den), jnp.exp(-m_t))[..., None]
        b_end = b[..., -1]
        loc = b_end[..., None] - b + ic
        m_new = jnp.maximum(b_end + m, loc.max(axis=-1))
        w = jnp.exp(loc - m_new[..., None])
        decay = jnp.exp(b_end + m - m_new)
        C = decay[..., None, None] * C + jnp.einsum('bhj,bhjd,bhjv->bhdv', w, kc, vc)
        n = decay[..., None] * n + jnp.einsum('bhj,bhjd->bhd', w, kc)
        return (C, n, m_new), h

    state, hs = lax.scan(step, state0, tuple(to_chunks(a) for a in (q, k, v, ig, lf)))
    return from_chunks(hs), state


def run_bidirectional(scan_f, scan_b, init, ctx_f, lat_f, ctx_b, lat_b):
    rev = lambda xs: tuple(jnp.flip(a, axis=2) for a in xs)
    yc_f, st_f = scan_f(*ctx_f, init)
    yl_f, _ = scan_f(*lat_f, st_f)
    yc_b, st_b = scan_b(*rev(ctx_b), init)
    yl_b, _ = scan_b(*rev(lat_b), st_b)
    return yc_f + jnp.flip(yc_b, axis=2), yl_f + jnp.flip(yl_b, axis=2)


def project_stream(h, w_in, conv_w, conv_b, gate_b, rope_pos):
    B, T, _ = h.shape
    rq, rk, rv, rg, mq, mk, mv, mo, mg, merge = split_cols(h @ w_in, IN_SIZES)
    mqk = jax.nn.silu(dwconv(jnp.concatenate([mq, mk], axis=-1), conv_w, conv_b))
    mq, mk = mqk[..., :M_QK], mqk[..., M_QK:]
    rq = split_heads(rq, H_R) * (R_DK ** -0.5)
    rk = split_heads(rk, H_R)
    if rope_pos is not None:
        rq = axial_rope(rq, *rope_pos)
        rk = axial_rope(rk, *rope_pos)
    ret = (rq, rk, split_heads(rv, H_R))
    gates = (mg.astype(jnp.float32) + gate_b.astype(jnp.float32).reshape(-1))
    gates = gates.reshape(B, T, 4, H_M).transpose(2, 0, 3, 1)
    mq_h = split_heads(mq, H_M) * (M_DK ** -0.5)
    mk_h = split_heads(mk, H_M)
    mv_h = split_heads(mv, H_M)
    ml_f = (mq_h, mk_h, mv_h, gates[0], jax.nn.log_sigmoid(gates[1]))
    ml_b = (mq_h, mk_h, mv_h, gates[2], jax.nn.log_sigmoid(gates[3]))
    return ret, ml_f, ml_b, (rg, mo, merge)


def branch_merge(yr, ym, gates, head_norm_w, w_ret_out, w_mlstm_out, w_o, dtype):
    rg, mo, merge = gates
    yr = merge_heads(head_rms(yr, head_norm_w[0])).astype(dtype) * jax.nn.silu(rg)
    ym = merge_heads(head_rms(ym, head_norm_w[1])).astype(dtype) * jax.nn.sigmoid(mo)
    gr, gm = jnp.split(merge, 2, axis=-1)
    y = jax.nn.sigmoid(gr) * (yr @ w_ret_out) + jax.nn.sigmoid(gm) * (ym @ w_mlstm_out)
    return y @ w_o


def token_mixer(hc, hl, w_in, conv_w, conv_b, gate_b, decay_exp, head_norm_w,
                w_ret_out, w_mlstm_out, w_o, rope_pos, with_ctx_out):
    ret_c, mf_c, mb_c, g_c = project_stream(hc, w_in, conv_w, conv_b, gate_b, None)
    ret_l, mf_l, mb_l, g_l = project_stream(hl, w_in, conv_w, conv_b, gate_b, rope_pos)
    B = hl.shape[0]
    log_gamma = jnp.log1p(-jnp.exp2(-decay_exp.astype(jnp.float32)))
    r_init = jnp.zeros((B, H_R, R_DK, R_DV), jnp.float32)
    yr_c, yr_l = run_bidirectional(functools.partial(retention_scan, log_gamma=log_gamma[0]),
                                   functools.partial(retention_scan, log_gamma=log_gamma[1]),
                                   r_init, ret_c, ret_l, ret_c, ret_l)
    m_init = (jnp.zeros((B, H_M, M_DK, M_DV), jnp.float32),
              jnp.zeros((B, H_M, M_DK), jnp.float32),
              jnp.zeros((B, H_M), jnp.float32))
    ym_c, ym_l = run_bidirectional(mlstm_scan, mlstm_scan, m_init, mf_c, mf_l, mb_c, mb_l)
    out_l = branch_merge(yr_l, ym_l, g_l, head_norm_w, w_ret_out, w_mlstm_out, w_o, hl.dtype)
    out_c = None
    if with_ctx_out:
        out_c = branch_merge(yr_c, ym_c, g_c, head_norm_w, w_ret_out, w_mlstm_out, w_o, hc.dtype)
    return out_c, out_l


def conv_ffn(h, w_up, conv_w, conv_b, w_down):
    u = dwconv(h @ w_up, conv_w, conv_b)
    a, g = jnp.split(u, 2, axis=-1)
    return (jax.nn.silu(a) * g) @ w_down


def setup_inputs(seed: int = 0) -> dict:
    key = jax.random.key(seed)
    ks = jax.random.split(key, 20)
    f32 = jnp.float32
    nrm = lambda k, shape, s: jax.random.normal(k, shape, f32) * s
    gate_base = jnp.stack([jnp.zeros((H_M,), f32), jnp.linspace(3.0, 6.0, H_M, dtype=f32),
                           jnp.zeros((H_M,), f32), jnp.linspace(3.0, 6.0, H_M, dtype=f32)])
    decay_base = 5.0 + jnp.arange(H_R, dtype=f32)
    return {
        'x': nrm(ks[0], (BATCH, SEQ, D_MODEL), 1.0),
        'c': nrm(ks[1], (BATCH, D_MODEL), 1.0),
        'ctx': nrm(ks[2], (BATCH, CTX_LEN, D_MODEL), 1.0),
        'c_ctx': nrm(ks[3], (D_MODEL,), 1.0),
        'w_ada': nrm(ks[4], (DEPTH, D_MODEL, N_MOD * D_MODEL), 0.5 * D_MODEL ** -0.5),
        'b_ada': nrm(ks[5], (DEPTH, N_MOD * D_MODEL), 0.02),
        'norm_w': 1.0 + nrm(ks[6], (DEPTH, 4, D_MODEL), 0.02),
        'w_in': nrm(ks[7], (DEPTH, D_MODEL, IN_COLS), D_MODEL ** -0.5),
        'mlstm_conv_w': nrm(ks[8], (DEPTH, CONV_W, 2 * M_QK), CONV_W ** -0.5),
        'mlstm_conv_b': nrm(ks[9], (DEPTH, 2 * M_QK), 0.02),
        'mlstm_gate_b': gate_base[None] + nrm(ks[10], (DEPTH, 4, H_M), 0.1),
        'ret_decay_exp': decay_base[None, None] + nrm(ks[11], (DEPTH, 2, H_R), 0.1),
        'head_norm_w': 1.0 + nrm(ks[12], (DEPTH, 2, VW), 0.02),
        'w_ret_out': nrm(ks[13], (DEPTH, VW, D_MODEL), VW ** -0.5),
        'w_mlstm_out': nrm(ks[14], (DEPTH, VW, D_MODEL), VW ** -0.5),
        'w_o': nrm(ks[15], (DEPTH, D_MODEL, D_MODEL), D_MODEL ** -0.5),
        'w_up': nrm(ks[16], (DEPTH, D_MODEL, 2 * D_FF), D_MODEL ** -0.5),
        'ffn_conv_w': nrm(ks[17], (DEPTH, CONV_W, 2 * D_FF), CONV_W ** -0.5),
        'ffn_conv_b': nrm(ks[18], (DEPTH, 2 * D_FF), 0.02),
        'w_down': nrm(ks[19], (DEPTH, D_FF, D_MODEL), D_FF ** -0.5),
    }


def reference(x, c, ctx, c_ctx, w_ada, b_ada, norm_w, w_in, mlstm_conv_w, mlstm_conv_b,
              mlstm_gate_b, ret_decay_exp, head_norm_w, w_ret_out, w_mlstm_out, w_o,
              w_up, ffn_conv_w, ffn_conv_b, w_down):
    T = x.shape[1]
    ROWS = T // GRID_W
    rows = jnp.repeat(jnp.arange(ROWS, dtype=jnp.float32), GRID_W)
    cols = jnp.tile(jnp.arange(GRID_W, dtype=jnp.float32), ROWS)
    s_c = jax.nn.silu(c)
    s_cc = jax.nn.silu(c_ctx)
    xl, xc = x, ctx
    for l in range(DEPTH):
        last = l == DEPTH - 1
        sh1, sc1, g1, sh2, sc2, g2 = jnp.split((s_c @ w_ada[l] + b_ada[l])[:, None, :], N_MOD, axis=-1)
        csh1, csc1, cg1, csh2, csc2, cg2 = jnp.split(s_cc @ w_ada[l] + b_ada[l], N_MOD, axis=-1)
        hl = modulate(rms_norm(xl, norm_w[l, 0]), sh1, sc1)
        hc = modulate(rms_norm(xc, norm_w[l, 0]), csh1, csc1)
        out_c, out_l = token_mixer(hc, hl, w_in[l], mlstm_conv_w[l], mlstm_conv_b[l], mlstm_gate_b[l],
                                   ret_decay_exp[l], head_norm_w[l], w_ret_out[l], w_mlstm_out[l], w_o[l],
                                   (rows, cols), not last)
        xl = xl + g1 * rms_norm(out_l, norm_w[l, 1])
        hl = modulate(rms_norm(xl, norm_w[l, 2]), sh2, sc2)
        xl = xl + g2 * rms_norm(conv_ffn(hl, w_up[l], ffn_conv_w[l], ffn_conv_b[l], w_down[l]), norm_w[l, 3])
        if not last:
            xc = xc + cg1 * rms_norm(out_c, norm_w[l, 1])
            hc = modulate(rms_norm(xc, norm_w[l, 2]), csh2, csc2)
            xc = xc + cg2 * rms_norm(conv_ffn(hc, w_up[l], ffn_conv_w[l], ffn_conv_b[l], w_down[l]), norm_w[l, 3])
    return xl
```

```python
import functools
import math

import jax
import jax.numpy as jnp
from jax import lax
from jax.experimental import pallas as pl
from jax.experimental.pallas import tpu as pltpu

F32 = jnp.float32
BF16 = jnp.bfloat16

GRID_W = 64
CHUNK = 128
H_R = 8
H_M = 8
N_MOD = 6
CONV_W = 3
ROPE_BASE = 10000.0
EPS = 1e-6
HALO = 16
V7X_VMEM_LIMIT = 56 * 1024 * 1024

BM_PROJ = 1024
BN_PROJ = 1024
BM_CONV = 1024
BN_CONV = 512
BM_ROW = 256
BM_NORM = 512
BN_ADA = 1024


def _tile(n, pref, quantum):
    if n <= pref:
        return n
    t = (pref // quantum) * quantum
    while t > quantum and n % t:
        t -= quantum
    assert n % t == 0, (n, pref, quantum)
    return t


def _params(*sem):
    return pltpu.CompilerParams(dimension_semantics=sem, vmem_limit_bytes=V7X_VMEM_LIMIT)


def _sigmoid(x):
    return 1.0 / (1.0 + jnp.exp(-x))


def _silu(x):
    return x * _sigmoid(x)


def _log_sigmoid(x):
    return jnp.minimum(x, 0.0) - jnp.log1p(jnp.exp(-jnp.abs(x)))


def _rms(x, g):
    ms = jnp.mean(x * x, axis=-1, keepdims=True)
    return x * lax.rsqrt(ms + EPS) * g


def _normmod(x, g, shift, scale):
    return _rms(x, g) * (1.0 + scale) + shift


def _dot(a, b):
    return jnp.dot(a, b, preferred_element_type=F32)


def _dot_nt(a, b):
    return lax.dot_general(a, b, (((1,), (1,)), ((), ())), preferred_element_type=F32)


def _dot_tn(a, b):
    return lax.dot_general(a, b, (((0,), (0,)), ((), ())), preferred_element_type=F32)


def _ada_kernel(c_ref, w_ref, b_ref, o_ref):
    s = _silu(c_ref[...]).astype(BF16)
    o_ref[0] = _dot(s, w_ref[0].astype(BF16)) + b_ref[0]


def _ada(cvec, w_ada, b_ada):
    depth, d, n = w_ada.shape
    r = cvec.shape[0]
    bn = _tile(n, BN_ADA, 128)
    return pl.pallas_call(
        _ada_kernel,
        grid=(depth, n // bn),
        in_specs=[pl.BlockSpec((r, d), lambda l, j: (0, 0)),
                  pl.BlockSpec((1, d, bn), lambda l, j: (l, 0, j)),
                  pl.BlockSpec((1, 1, bn), lambda l, j: (l, 0, j))],
        out_specs=pl.BlockSpec((1, r, bn), lambda l, j: (l, 0, j)),
        out_shape=jax.ShapeDtypeStruct((depth, r, n), F32),
        compiler_params=_params("arbitrary", "arbitrary"),
        name="ada",
    )(cvec, w_ada, b_ada.reshape(depth, 1, n))


def _normmod_kernel(x_ref, g_ref, sh_ref, sc_ref, o_ref):
    o_ref[...] = _normmod(x_ref[...], g_ref[...], sh_ref[0], sc_ref[0]).astype(o_ref.dtype)


def _normmod_call(x, g, shift, scale, rows_per_mod):
    m, d = x.shape
    bm = _tile(rows_per_mod, BM_NORM, 16)
    per = rows_per_mod // bm
    vec = pl.BlockSpec((1, 1, d), lambda i: (i // per, 0, 0))
    return pl.pallas_call(
        _normmod_kernel,
        grid=(m // bm,),
        in_specs=[pl.BlockSpec((bm, d), lambda i: (i, 0)),
                  pl.BlockSpec((1, d), lambda i: (0, 0)), vec, vec],
        out_specs=pl.BlockSpec((bm, d), lambda i: (i, 0)),
        out_shape=jax.ShapeDtypeStruct((m, d), BF16),
        compiler_params=_params("parallel"),
        name="normmod",
    )(x, g.reshape(1, d), shift, scale)


def _proj_kernel(h_ref, w_ref, o_ref):
    o_ref[...] = _dot(h_ref[...], w_ref[...]).astype(o_ref.dtype)


def _proj_scale_kernel(h_ref, w_ref, s_ref, o_ref):
    o_ref[...] = (_dot(h_ref[...], w_ref[...]) * s_ref[...]).astype(o_ref.dtype)


def _swap32(r):
    lane = lax.broadcasted_iota(jnp.int32, r.shape, 1)
    return jnp.where((lane & 32) == 0, pltpu.roll(r, 96, axis=1), pltpu.roll(r, 32, axis=1))


def _proj_rope_kernel(h_ref, w_ref, c_ref, s_ref, o_ref):
    res = _dot(h_ref[...], w_ref[...])
    cos, sin = c_ref[0], s_ref[0]
    for g in range(res.shape[1] // 128):
        r = res[:, g * 128:(g + 1) * 128]
        o_ref[:, g * 128:(g + 1) * 128] = (r * cos + _swap32(r) * sin).astype(o_ref.dtype)


def _proj(h, w, out_dtype=None, colscale=None, rope=None, seq=None, name="proj"):
    out_dtype = out_dtype or BF16
    m, d = h.shape
    n = w.shape[1]
    bm = _tile(m if seq is None else seq, BM_PROJ, 16)
    bn = _tile(n, BN_PROJ, 128) if rope is None else n // 2
    in_specs = [pl.BlockSpec((bm, d), lambda i, j: (i, 0)),
                pl.BlockSpec((d, bn), lambda i, j: (0, j))]
    args = [h, w]
    kern = _proj_kernel
    if colscale is not None:
        kern = _proj_scale_kernel
        in_specs.append(pl.BlockSpec((1, bn), lambda i, j: (0, j)))
        args.append(colscale.reshape(1, n))
    if rope is not None:
        kern = _proj_rope_kernel
        cos, sin = rope
        assert cos.shape == (n // bn, seq, 128)
        per = seq // bm
        tab = pl.BlockSpec((1, bm, 128), lambda i, j: (j, i % per, 0))
        in_specs += [tab, tab]
        args += [cos, sin]
    return pl.pallas_call(
        kern,
        grid=(m // bm, n // bn),
        in_specs=in_specs,
        out_specs=pl.BlockSpec((bm, bn), lambda i, j: (i, j)),
        out_shape=jax.ShapeDtypeStruct((m, n), out_dtype),
        compiler_params=_params("parallel", "arbitrary"),
        name=name,
    )(*args)


def _fill_halo(hp_ref, hc_ref, hn_ref, hext, first, last, bm):
    hext[0:HALO, :] = hp_ref[...]
    hext[HALO:HALO + bm, :] = hc_ref[...]
    hext[HALO + bm:, :] = hn_ref[...]

    @pl.when(first)
    def _():
        hext[0:HALO, :] = jnp.zeros((HALO, hext.shape[1]), hext.dtype)

    @pl.when(last)
    def _():
        hext[HALO + bm:, :] = jnp.zeros((HALO, hext.shape[1]), hext.dtype)


def _conv3(u, cw, cb, bm):
    prev = u[HALO - 1:HALO - 1 + bm]
    cur = u[HALO:HALO + bm]
    nxt = u[HALO + 1:HALO + 1 + bm]
    return cb + prev * cw[0:1] + cur * cw[1:2] + nxt * cw[2:3]


def _convproj_silu_kernel(hp_ref, hc_ref, hn_ref, w_ref, cw_ref, cb_ref, s_ref, o_ref, hext,
                          *, bm, per):
    i, j = pl.program_id(0), pl.program_id(1)

    @pl.when(j == 0)
    def _():
        _fill_halo(hp_ref, hc_ref, hn_ref, hext, i % per == 0, i % per == per - 1, bm)

    u = _dot(hext[...], w_ref[...])
    a = _conv3(u, cw_ref[...], cb_ref[...], bm)
    o_ref[...] = (_silu(a) * s_ref[...]).astype(o_ref.dtype)


def _convproj_glu_kernel(hp_ref, hc_ref, hn_ref, wa_ref, wg_ref, cwa_ref, cba_ref, cwg_ref, cbg_ref,
                         o_ref, hext, *, bm, per):
    i, j = pl.program_id(0), pl.program_id(1)

    @pl.when(j == 0)
    def _():
        _fill_halo(hp_ref, hc_ref, hn_ref, hext, i % per == 0, i % per == per - 1, bm)

    hx = hext[...]
    a = _conv3(_dot(hx, wa_ref[...]), cwa_ref[...], cba_ref[...], bm)
    g = _conv3(_dot(hx, wg_ref[...]), cwg_ref[...], cbg_ref[...], bm)
    o_ref[...] = (_silu(a) * g).astype(o_ref.dtype)


def _convproj(h, w, cw, cb, seq, colscale=None, glu=False, name="convproj"):
    m, d = h.shape
    n = w.shape[1] // (2 if glu else 1)
    bm = _tile(seq, BM_CONV, HALO)
    bn = _tile(n, BN_CONV, 128)
    per = seq // bm
    nb = bm // HALO
    last_blk = m // HALO - 1
    hspecs = [pl.BlockSpec((HALO, d), lambda i, j: (jnp.maximum(i * nb - 1, 0), 0)),
              pl.BlockSpec((bm, d), lambda i, j: (i, 0)),
              pl.BlockSpec((HALO, d), lambda i, j: (jnp.minimum((i + 1) * nb, last_blk), 0))]
    wspec = lambda off: pl.BlockSpec((d, bn), lambda i, j: (0, j + off))
    cwspec = lambda off: pl.BlockSpec((CONV_W, bn), lambda i, j: (0, j + off))
    cbspec = lambda off: pl.BlockSpec((1, bn), lambda i, j: (0, j + off))
    cb2 = cb.reshape(1, -1)
    if glu:
        off = n // bn
        kern = functools.partial(_convproj_glu_kernel, bm=bm, per=per)
        in_specs = hspecs + [wspec(0), wspec(off), cwspec(0), cbspec(0), cwspec(off), cbspec(off)]
        args = [h, h, h, w, w, cw, cb2, cw, cb2]
    else:
        kern = functools.partial(_convproj_silu_kernel, bm=bm, per=per)
        in_specs = hspecs + [wspec(0), cwspec(0), cbspec(0), cbspec(0)]
        args = [h, h, h, w, cw, cb2, colscale.reshape(1, n)]
    return pl.pallas_call(
        kern,
        grid=(m // bm, n // bn),
        in_specs=in_specs,
        out_specs=pl.BlockSpec((bm, bn), lambda i, j: (i, j)),
        out_shape=jax.ShapeDtypeStruct((m, n), BF16),
        scratch_shapes=[pltpu.VMEM((bm + 2 * HALO, d), BF16)],
        compiler_params=_params("parallel", "arbitrary"),
        name=name,
    )(*args)


def _head_norm_store(y_ref, g, o_ref, n_chunks):
    def body(c, carry):
        sl = pl.ds(pl.multiple_of(c * CHUNK, CHUNK), CHUNK)
        o_ref[0, sl, :] = _rms(y_ref[sl, :], g).astype(o_ref.dtype)
        return carry
    lax.fori_loop(0, n_chunks, body, 0)


def _retention_kernel(lg_ref, qc_ref, kc_ref, vc_ref, ql_ref, kl_ref, vl_ref, g_ref,
                      oc_ref, ol_ref, yc, yl):
    h = pl.program_id(1)
    L = CHUNK
    lgf, lgb = lg_ref[0, h], lg_ref[1, h]
    row = lax.broadcasted_iota(jnp.int32, (L, L), 0).astype(F32)
    col = lax.broadcasted_iota(jnp.int32, (L, L), 1).astype(F32)
    rel = row - col
    mask = (jnp.where(rel >= 0, jnp.exp(jnp.maximum(rel, 0.0) * lgf), 0.0)
            + jnp.where(rel <= 0, jnp.exp(jnp.maximum(-rel, 0.0) * lgb), 0.0))
    qdec_f = jnp.exp((row + 1.0) * lgf)
    kdec_f = jnp.exp((L - 1.0 - row) * lgf)
    qdec_b = jnp.exp((L - row) * lgb)
    kdec_b = jnp.exp(row * lgb)
    dk = qc_ref.shape[-1]
    dv = vc_ref.shape[-1]
    cdec_f = jnp.exp(jnp.full((1, dv), L, F32) * lgf)
    cdec_b = jnp.exp(jnp.full((1, dv), L, F32) * lgb)
    assert dk == L

    def fwd_step(q_ref, k_ref, v_ref, y_ref):
        def body(c, S):
            sl = pl.ds(pl.multiple_of(c * L, L), L)
            q, k, v = q_ref[0, sl, :], k_ref[0, sl, :], v_ref[0, sl, :]
            s = (_dot_nt(q, k) * mask).astype(BF16)
            qd = (q.astype(F32) * qdec_f).astype(BF16)
            y_ref[sl, :] = _dot(s, v) + _dot(qd, S.astype(BF16))
            kd = (k.astype(F32) * kdec_f).astype(BF16)
            return cdec_f * S + _dot_tn(kd, v)
        return body

    def bwd_step(q_ref, k_ref, v_ref, y_ref, n):
        def body(t, S):
            c = n - 1 - t
            sl = pl.ds(pl.multiple_of(c * L, L), L)
            q, k, v = q_ref[0, sl, :], k_ref[0, sl, :], v_ref[0, sl, :]
            qd = (q.astype(F32) * qdec_b).astype(BF16)
            y_ref[sl, :] += _dot(qd, S.astype(BF16))
            kd = (k.astype(F32) * kdec_b).astype(BF16)
            return cdec_b * S + _dot_tn(kd, v)
        return body

    nc = qc_ref.shape[1] // L
    nl = ql_ref.shape[1] // L
    S0 = jnp.zeros((dk, dv), F32)
    S = lax.fori_loop(0, nc, fwd_step(qc_ref, kc_ref, vc_ref, yc), S0)
    lax.fori_loop(0, nl, fwd_step(ql_ref, kl_ref, vl_ref, yl), S)
    S = lax.fori_loop(0, nc, bwd_step(qc_ref, kc_ref, vc_ref, yc, nc), S0)
    lax.fori_loop(0, nl, bwd_step(ql_ref, kl_ref, vl_ref, yl, nl), S)
    g = g_ref[...]
    _head_norm_store(yc, g, oc_ref, nc)
    _head_norm_store(yl, g, ol_ref, nl)


def _retention(log_gamma, qk_c, pv_c, qk_l, pv_l, g, v_blk, batch):
    tc, tl = qk_c.shape[1], qk_l.shape[1]
    dk = qk_c.shape[2] // (2 * H_R)
    dv = g.shape[0] // H_R
    qs = lambda t: pl.BlockSpec((1, t, dk), lambda b, h: (b, 0, h))
    ks = lambda t: pl.BlockSpec((1, t, dk), lambda b, h: (b, 0, H_R + h))
    vs = lambda t: pl.BlockSpec((1, t, dv), lambda b, h: (b, 0, v_blk + h))
    os_ = lambda t: pl.BlockSpec((1, t, dv), lambda b, h: (b, 0, h))
    return pl.pallas_call(
        _retention_kernel,
        grid=(batch, H_R),
        in_specs=[pl.BlockSpec(memory_space=pltpu.SMEM),
                  qs(tc), ks(tc), vs(tc), qs(tl), ks(tl), vs(tl),
                  pl.BlockSpec((1, dv), lambda b, h: (0, h))],
        out_specs=[os_(tc), os_(tl)],
        out_shape=[jax.ShapeDtypeStruct((batch, tc, H_R * dv), BF16),
                   jax.ShapeDtypeStruct((batch, tl, H_R * dv), BF16)],
        scratch_shapes=[pltpu.VMEM((tc, dv), F32), pltpu.VMEM((tl, dv), F32)],
        compiler_params=_params("parallel", "arbitrary"),
        name="retention",
    )(log_gamma, qk_c, qk_c, pv_c, qk_l, qk_l, pv_l, g.reshape(1, -1))


def _lane_cumsum(x, reverse):
    n = x.shape[-1]
    lane = lax.broadcasted_iota(jnp.int32, x.shape, 1)
    s = 1
    while s < n:
        if reverse:
            x = x + jnp.where(lane < n - s, pltpu.roll(x, n - s, axis=1), 0.0)
        else:
            x = x + jnp.where(lane >= s, pltpu.roll(x, s, axis=1), 0.0)
        s *= 2
    return x


def _mlstm_kernel(gb_ref, qc_ref, kc_ref, vc_ref, gic_ref, gfc_ref, gibc_ref, gfbc_ref,
                  ql_ref, kl_ref, vl_ref, gil_ref, gfl_ref, gibl_ref, gfbl_ref, g_ref,
                  oc_ref, ol_ref, yc, yl, bc, ic, bl, il):
    h = pl.program_id(1)
    L = CHUNK
    dk = qc_ref.shape[-1]
    dv = vc_ref.shape[-1]
    assert dk == L
    row = lax.broadcasted_iota(jnp.int32, (L, L), 0)
    col = lax.broadcasted_iota(jnp.int32, (L, L), 1)
    lane1 = lax.broadcasted_iota(jnp.int32, (1, L), 1)

    def sweep(q_ref, k_ref, v_ref, gi_ref, gf_ref, y_ref, b_scr, i_scr, reverse, state, first):
        n = q_ref.shape[1] // L
        gsel = 2 if reverse else 0
        i_scr[...] = gi_ref[0, 0] + gb_ref[gsel, h]
        b_scr[...] = _lane_cumsum(_log_sigmoid(gf_ref[0, 0] + gb_ref[gsel + 1, h]), reverse)
        tri = (col >= row) if reverse else (col <= row)
        end_lane = 0 if reverse else L - 1

        def body(t, carry):
            C, nvec, m = carry
            c = (n - 1 - t) if reverse else t
            sl = pl.ds(pl.multiple_of(c * L, L), L)
            q, k, v = q_ref[0, sl, :], k_ref[0, sl, :], v_ref[0, sl, :]
            brow = b_scr[pl.ds(c, 1), :]
            irow = i_scr[pl.ds(c, 1), :]
            Br = jnp.broadcast_to(brow, (L, L))
            Bc = Br.T
            d = jnp.where(tri, Bc - Br + irow, -jnp.inf)
            inter = Bc[:, 0:1] + m
            m_t = jnp.maximum(jnp.max(d, axis=-1, keepdims=True), inter)
            s = _dot_nt(q, k) * jnp.exp(d - m_t)
            a = jnp.exp(inter - m_t)
            qf = q.astype(F32)
            num = _dot(s.astype(BF16), v) + a * _dot(q, C.astype(BF16))
            den = jnp.sum(s, axis=-1, keepdims=True) + a * jnp.sum(qf * nvec, axis=-1, keepdims=True)
            hval = num / jnp.maximum(jnp.abs(den), jnp.exp(-m_t))
            if first:
                y_ref[sl, :] = hval
            else:
                y_ref[sl, :] += hval
            b_end = jnp.sum(jnp.where(lane1 == end_lane, brow, 0.0), axis=-1, keepdims=True)
            loc = b_end - brow + irow
            m_new = jnp.maximum(b_end + m, jnp.max(loc, axis=-1, keepdims=True))
            w = jnp.exp(loc - m_new)
            decay = jnp.exp(b_end + m - m_new)
            Wc = jnp.broadcast_to(w, (L, L)).T
            kw = k.astype(F32) * Wc
            C = decay * C + _dot_tn(kw.astype(BF16), v)
            nvec = decay * nvec + jnp.sum(kw, axis=0, keepdims=True)
            return C, nvec, m_new

        return lax.fori_loop(0, n, body, state)

    zero = (jnp.zeros((dk, dv), F32), jnp.zeros((1, dk), F32), jnp.zeros((1, 1), F32))
    st = sweep(qc_ref, kc_ref, vc_ref, gic_ref, gfc_ref, yc, bc, ic, False, zero, True)
    sweep(ql_ref, kl_ref, vl_ref, gil_ref, gfl_ref, yl, bl, il, False, st, True)
    st = sweep(qc_ref, kc_ref, vc_ref, gibc_ref, gfbc_ref, yc, bc, ic, True, zero, False)
    sweep(ql_ref, kl_ref, vl_ref, gibl_ref, gfbl_ref, yl, bl, il, True, st, False)
    g = g_ref[...]
    _head_norm_store(yc, g, oc_ref, qc_ref.shape[1] // L)
    _head_norm_store(yl, g, ol_ref, ql_ref.shape[1] // L)


def _mlstm(gate_b, qk_c, pv_c, gates_c, qk_l, pv_l, gates_l, g, v_blk, batch):
    tc, tl = qk_c.shape[1], qk_l.shape[1]
    dk = qk_c.shape[2] // (2 * H_M)
    dv = g.shape[0] // H_M
    qs = lambda t: pl.BlockSpec((1, t, dk), lambda b, h: (b, 0, h))
    ks = lambda t: pl.BlockSpec((1, t, dk), lambda b, h: (b, 0, H_M + h))
    vs = lambda t: pl.BlockSpec((1, t, dv), lambda b, h: (b, 0, v_blk + h))
    gs = lambda t, k: pl.BlockSpec((1, 1, t // CHUNK, CHUNK), lambda b, h: (b, k * H_M + h, 0, 0))
    os_ = lambda t: pl.BlockSpec((1, t, dv), lambda b, h: (b, 0, h))

    def stream(t):
        return [qs(t), ks(t), vs(t), gs(t, 0), gs(t, 1), gs(t, 2), gs(t, 3)]

    return pl.pallas_call(
        _mlstm_kernel,
        grid=(batch, H_M),
        in_specs=[pl.BlockSpec(memory_space=pltpu.SMEM)] + stream(tc) + stream(tl)
                 + [pl.BlockSpec((1, dv), lambda b, h: (0, h))],
        out_specs=[os_(tc), os_(tl)],
        out_shape=[jax.ShapeDtypeStruct((batch, tc, H_M * dv), BF16),
                   jax.ShapeDtypeStruct((batch, tl, H_M * dv), BF16)],
        scratch_shapes=[pltpu.VMEM((tc, dv), F32), pltpu.VMEM((tl, dv), F32),
                        pltpu.VMEM((tc // CHUNK, CHUNK), F32), pltpu.VMEM((tc // CHUNK, CHUNK), F32),
                        pltpu.VMEM((tl // CHUNK, CHUNK), F32), pltpu.VMEM((tl // CHUNK, CHUNK), F32)],
        compiler_params=_params("parallel", "arbitrary"),
        name="mlstm",
    )(gate_b, qk_c, qk_c, pv_c, gates_c, gates_c, gates_c, gates_c,
      qk_l, qk_l, pv_l, gates_l, gates_l, gates_l, gates_l, g.reshape(1, -1))


def _merge_kernel(yr_ref, rg_ref, ym_ref, mo_ref, gr_ref, gm_ref, wr_ref, wm_ref, o_ref):
    a = (yr_ref[...] * _silu(rg_ref[...].astype(F32)).astype(BF16))
    b = (ym_ref[...] * _sigmoid(mo_ref[...].astype(F32)).astype(BF16))
    y = (_sigmoid(gr_ref[...].astype(F32)) * _dot(a, wr_ref[...])
         + _sigmoid(gm_ref[...].astype(F32)) * _dot(b, wm_ref[...]))
    o_ref[...] = y.astype(o_ref.dtype)


def _merge(yr, ym, pv, wr, wm, blk):
    m, vw = yr.shape
    d = wr.shape[1]
    bm = _tile(m, BM_ROW, 16)
    col = lambda k: pl.BlockSpec((bm, vw), lambda i: (i, k))
    wspec = pl.BlockSpec((vw, d), lambda i: (0, 0), pipeline_mode=pl.Buffered(1))
    return pl.pallas_call(
        _merge_kernel,
        grid=(m // bm,),
        in_specs=[col(0), col(blk[0]), col(0), col(blk[1]), col(blk[2]), col(blk[3]), wspec, wspec],
        out_specs=pl.BlockSpec((bm, d), lambda i: (i, 0)),
        out_shape=jax.ShapeDtypeStruct((m, d), BF16),
        compiler_params=_params("parallel"),
        name="merge",
    )(yr, pv, ym, pv, pv, pv, wr, wm)


def _resid_kernel(a_ref, w_ref, x_ref, go_ref, gate_ref, gn_ref, sh_ref, sc_ref, xo_ref, ho_ref):
    out = _dot(a_ref[...], w_ref[...])
    xn = x_ref[...] + gate_ref[0] * _rms(out, go_ref[...])
    xo_ref[...] = xn
    ho_ref[...] = _normmod(xn, gn_ref[...], sh_ref[0], sc_ref[0]).astype(ho_ref.dtype)


def _resid_last_kernel(a_ref, w_ref, x_ref, go_ref, gate_ref, xo_ref):
    out = _dot(a_ref[...], w_ref[...])
    xo_ref[...] = x_ref[...] + gate_ref[0] * _rms(out, go_ref[...])


def _resid(a, w, x, g_out, gate, rows_per_mod, nxt=None, name="resid"):
    m, k = a.shape
    d = w.shape[1]
    bm = _tile(rows_per_mod, BM_ROW, 16)
    per = rows_per_mod // bm
    row = lambda width: pl.BlockSpec((bm, width), lambda i: (i, 0))
    vec = pl.BlockSpec((1, 1, d), lambda i: (i // per, 0, 0))
    gvec = pl.BlockSpec((1, d), lambda i: (0, 0))
    in_specs = [row(k), pl.BlockSpec((k, d), lambda i: (0, 0), pipeline_mode=pl.Buffered(1)),
                row(d), gvec, vec]
    args = [a, w, x, g_out.reshape(1, d), gate]
    if nxt is None:
        return pl.pallas_call(
            _resid_last_kernel, grid=(m // bm,), in_specs=in_specs, out_specs=row(d),
            out_shape=jax.ShapeDtypeStruct((m, d), F32),
            compiler_params=_params("parallel"), name=name,
        )(*args)
    g_next, shift, scale = nxt
    return pl.pallas_call(
        _resid_kernel, grid=(m // bm,),
        in_specs=in_specs + [gvec, vec, vec],
        out_specs=[row(d), row(d)],
        out_shape=[jax.ShapeDtypeStruct((m, d), F32), jax.ShapeDtypeStruct((m, d), BF16)],
        compiler_params=_params("parallel"), name=name,
    )(*args, g_next.reshape(1, d), shift, scale)


def _rope_tables(seq, dk, q_scale):
    quarter = dk // 4
    t = jnp.arange(seq)
    rows = (t // GRID_W).astype(F32)
    cols = (t % GRID_W).astype(F32)
    freqs = ROPE_BASE ** (-jnp.arange(quarter, dtype=F32) / quarter)
    ar = rows[:, None] * freqs[None, :]
    ac = cols[:, None] * freqs[None, :]
    cos = jnp.concatenate([jnp.cos(ar), jnp.cos(ar), jnp.cos(ac), jnp.cos(ac)], axis=-1)
    sin = jnp.concatenate([-jnp.sin(ar), jnp.sin(ar), -jnp.sin(ac), jnp.sin(ac)], axis=-1)
    scale = jnp.array([q_scale, 1.0], F32)[:, None, None]
    return cos[None] * scale, sin[None] * scale


def _gate_rows(gates, batch, seq):
    g = gates[:, :4 * H_M].reshape(batch, seq, 4 * H_M)
    return jnp.transpose(g, (0, 2, 1)).reshape(batch, 4 * H_M, seq // CHUNK, CHUNK)


def kernel(x, c, ctx, c_ctx, w_ada, b_ada, norm_w, w_in, mlstm_conv_w, mlstm_conv_b, mlstm_gate_b,
           ret_decay_exp, head_norm_w, w_ret_out, w_mlstm_out, w_o, w_up, ffn_conv_w, ffn_conv_b,
           w_down):
    batch, seq, d = x.shape
    ctx_len = ctx.shape[1]
    depth = w_ada.shape[0]
    vw = w_ret_out.shape[1]
    r_qk = (w_in.shape[2] - 4 * vw - 4 * H_M - 2 * d) // 4
    r_dk = r_qk // H_R
    m_dk = r_qk // H_M
    assert d == vw, "column-block addressing of the projection output assumes D_MODEL == VW"

    o_rv = 2 * r_qk
    o_mq = o_rv + 2 * vw
    o_mv = o_mq + 2 * r_qk
    o_mg = o_mv + 2 * vw
    o_merge = o_mg + 4 * H_M

    pad = (-(batch + 1)) % 8
    cvec = jnp.concatenate([c, c_ctx[None], jnp.zeros((pad, d), F32)], axis=0)
    mod = _ada(cvec, w_ada, b_ada)

    def mod_l(l, k):
        return mod[l, :batch, k * d:(k + 1) * d].reshape(batch, 1, d)

    def mod_c(l, k):
        return mod[l, batch:batch + 1, k * d:(k + 1) * d].reshape(1, 1, d)

    rope = _rope_tables(seq, r_dk, r_dk ** -0.5)
    qscale_r = jnp.concatenate([jnp.full((r_qk,), r_dk ** -0.5, F32), jnp.ones((r_qk,), F32)])
    qscale_m = jnp.concatenate([jnp.full((r_qk,), m_dk ** -0.5, F32), jnp.ones((r_qk,), F32)])
    log_gamma = jnp.log1p(-jnp.exp2(-ret_decay_exp.astype(F32)))

    ml, mc = batch * seq, batch * ctx_len
    xl = x.reshape(ml, d)
    xc = ctx.reshape(mc, d)
    hl = _normmod_call(xl, norm_w[0, 0], mod_l(0, 0), mod_l(0, 1), seq)
    hc = _normmod_call(xc, norm_w[0, 0], mod_c(0, 0), mod_c(0, 1), mc)

    for l in range(depth):
        last = l == depth - 1
        wl = w_in[l].astype(BF16)
        w_rqk = wl[:, :o_rv]
        w_mqk = wl[:, o_mq:o_mv]
        w_plain = jnp.concatenate([wl[:, o_rv:o_mq], wl[:, o_mv:o_mg], wl[:, o_merge:]], axis=1)
        w_gate = jnp.pad(wl[:, o_mg:o_merge], ((0, 0), (0, 128 - 4 * H_M)))
        cw, cb = mlstm_conv_w[l], mlstm_conv_b[l]

        def project(h, seq_len, rope_tabs, nm):
            pv = _proj(h, w_plain, name=nm + "_plain")
            gates = _proj(h, w_gate, out_dtype=F32, name=nm + "_gates")
            if rope_tabs is None:
                rqk = _proj(h, w_rqk, colscale=qscale_r, name=nm + "_rqk")
            else:
                rqk = _proj(h, w_rqk, rope=rope_tabs, seq=seq_len, name=nm + "_rqk")
            mqk = _convproj(h, w_mqk, cw, cb, seq_len, colscale=qscale_m, name=nm + "_mqk")
            b = h.shape[0] // seq_len
            return (pv.reshape(b, seq_len, -1), rqk.reshape(b, seq_len, -1),
                    mqk.reshape(b, seq_len, -1), _gate_rows(gates, b, seq_len))

        pv_c, rqk_c, mqk_c, g_c = project(hc, ctx_len, None, "ctx")
        pv_l, rqk_l, mqk_l, g_l = project(hl, seq, rope, "lat")

        yr_c, yr_l = _retention(log_gamma[l], rqk_c, pv_c, rqk_l, pv_l, head_norm_w[l, 0],
                                0 * H_R, batch)
        ym_c, ym_l = _mlstm(mlstm_gate_b[l], mqk_c, pv_c, g_c, mqk_l, pv_l, g_l, head_norm_w[l, 1],
                            2 * H_M, batch)

        wr, wm, wo = (w.astype(BF16) for w in (w_ret_out[l], w_mlstm_out[l], w_o[l]))
        wu, wd = w_up[l].astype(BF16), w_down[l].astype(BF16)
        blk = (1, 3, 4, 5)

        def tail(yr, ym, pv, xres, mod_k, rows_per_mod, seq_len, nm):
            y = _merge(yr.reshape(-1, vw), ym.reshape(-1, vw), pv.reshape(yr.shape[0] * seq_len, -1),
                       wr, wm, blk)
            x1, h2 = _resid(y, wo, xres, norm_w[l, 1], mod_k(l, 2), rows_per_mod,
                            nxt=(norm_w[l, 2], mod_k(l, 3), mod_k(l, 4)), name=nm + "_o")
            act = _convproj(h2, wu, ffn_conv_w[l], ffn_conv_b[l], seq_len, glu=True, name=nm + "_ffn")
            if last:
                return _resid(act, wd, x1, norm_w[l, 3], mod_k(l, 5), rows_per_mod, name=nm + "_down"), None
            return _resid(act, wd, x1, norm_w[l, 3], mod_k(l, 5), rows_per_mod,
                          nxt=(norm_w[l + 1, 0], mod_k(l + 1, 0), mod_k(l + 1, 1)), name=nm + "_down")

        xl, hl = tail(yr_l, ym_l, pv_l, xl, mod_l, seq, seq, "lat")
        if not last:
            xc, hc = tail(yr_c, ym_c, pv_c, xc, mod_c, mc, ctx_len, "ctx")

    return xl.reshape(batch, seq, d)
```

```python
import functools

import jax
import jax.numpy as jnp
from jax import lax
from jax.experimental import pallas as pl
from jax.experimental.pallas import tpu as pltpu

F32 = jnp.float32
BF16 = jnp.bfloat16

GRID_W = 64
CHUNK = 128
H_R = 8
H_M = 8
N_MOD = 6
CONV_W = 3
ROPE_BASE = 10000.0
EPS = 1e-6
HALO = 16
MXU_COLS = 256
V7X_VMEM_LIMIT = 56 * 1024 * 1024

BM_PROJ = 1024
BN_PROJ = 1024
BM_CONV = 1024
BN_CONV = 512
BM_ROW = 256
BM_NORM = 512
BN_ADA = 1024
SCAN_UNROLL = 4


def _tile(n, pref, quantum):
    if n <= pref:
        return n
    t = (pref // quantum) * quantum
    while t > quantum and n % t:
        t -= quantum
    assert n % t == 0, (n, pref, quantum)
    return t


def _params(*sem):
    return pltpu.CompilerParams(dimension_semantics=sem, vmem_limit_bytes=V7X_VMEM_LIMIT)


def _sigmoid(x):
    return 1.0 / (1.0 + jnp.exp(-x))


def _silu(x):
    return x * _sigmoid(x)


def _log_sigmoid(x):
    return jnp.minimum(x, 0.0) - jnp.log1p(jnp.exp(-jnp.abs(x)))


def _rms(x, g):
    ms = jnp.mean(x * x, axis=-1, keepdims=True)
    return x * lax.rsqrt(ms + EPS) * g


def _normmod(x, g, shift, scale):
    return _rms(x, g) * (1.0 + scale) + shift


def _dot(a, b):
    return jnp.dot(a, b, preferred_element_type=F32)


def _dot_nt(a, b):
    return lax.dot_general(a, b, (((1,), (1,)), ((), ())), preferred_element_type=F32)


def _dot_tn(a, b):
    return lax.dot_general(a, b, (((0,), (0,)), ((), ())), preferred_element_type=F32)


def _ada_kernel(c_ref, w_ref, b_ref, o_ref):
    s = _silu(c_ref[...]).astype(BF16)
    o_ref[0] = _dot(s, w_ref[0].astype(BF16)) + b_ref[0]


def _ada(cvec, w_ada, b_ada):
    depth, d, n = w_ada.shape
    r = cvec.shape[0]
    bn = _tile(n, BN_ADA, 128)
    return pl.pallas_call(
        _ada_kernel,
        grid=(depth, n // bn),
        in_specs=[pl.BlockSpec((r, d), lambda l, j: (0, 0)),
                  pl.BlockSpec((1, d, bn), lambda l, j: (l, 0, j)),
                  pl.BlockSpec((1, 1, bn), lambda l, j: (l, 0, j))],
        out_specs=pl.BlockSpec((1, r, bn), lambda l, j: (l, 0, j)),
        out_shape=jax.ShapeDtypeStruct((depth, r, n), F32),
        compiler_params=_params("arbitrary", "arbitrary"),
        name="ada",
    )(cvec, w_ada, b_ada.reshape(depth, 1, n))


def _normmod_kernel(x_ref, g_ref, sh_ref, sc_ref, o_ref):
    o_ref[...] = _normmod(x_ref[...], g_ref[...], sh_ref[0], sc_ref[0]).astype(o_ref.dtype)


def _normmod_call(x, g, shift, scale, rows_per_mod):
    m, d = x.shape
    bm = _tile(rows_per_mod, BM_NORM, 16)
    per = rows_per_mod // bm
    vec = pl.BlockSpec((1, 1, d), lambda i: (i // per, 0, 0))
    return pl.pallas_call(
        _normmod_kernel,
        grid=(m // bm,),
        in_specs=[pl.BlockSpec((bm, d), lambda i: (i, 0)),
                  pl.BlockSpec((1, d), lambda i: (0, 0)), vec, vec],
        out_specs=pl.BlockSpec((bm, d), lambda i: (i, 0)),
        out_shape=jax.ShapeDtypeStruct((m, d), BF16),
        compiler_params=_params("parallel"),
        name="normmod",
    )(x, g.reshape(1, d), shift, scale)


def _proj_kernel(h_ref, w_ref, o_ref):
    o_ref[...] = _dot(h_ref[...], w_ref[...]).astype(o_ref.dtype)


def _proj_scale_kernel(h_ref, w_ref, s_ref, o_ref):
    o_ref[...] = (_dot(h_ref[...], w_ref[...]) * s_ref[...]).astype(o_ref.dtype)


def _swap32(r):
    lane = lax.broadcasted_iota(jnp.int32, r.shape, 1)
    return jnp.where((lane & 32) == 0, pltpu.roll(r, 96, axis=1), pltpu.roll(r, 32, axis=1))


def _subtiles(n):
    sub = MXU_COLS if n % MXU_COLS == 0 else n
    return [slice(s, s + sub) for s in range(0, n, sub)]


def _proj_rope_kernel(h_ref, w_ref, c_ref, s_ref, o_ref):
    h = h_ref[...]
    cos, sin = c_ref[0], s_ref[0]
    for cs in _subtiles(o_ref.shape[1]):
        res = _dot(h, w_ref[:, cs])
        for g in range(res.shape[1] // 128):
            r = res[:, g * 128:(g + 1) * 128]
            lo = cs.start + g * 128
            o_ref[:, lo:lo + 128] = (r * cos + _swap32(r) * sin).astype(o_ref.dtype)


def _proj(h, w, out_dtype=None, colscale=None, rope=None, seq=None, name="proj"):
    out_dtype = out_dtype or BF16
    m, d = h.shape
    n = w.shape[1]
    bm = _tile(m if seq is None else seq, BM_PROJ, 16)
    bn = _tile(n, BN_PROJ, 128) if rope is None else n // 2
    in_specs = [pl.BlockSpec((bm, d), lambda i, j: (i, 0)),
                pl.BlockSpec((d, bn), lambda i, j: (0, j))]
    args = [h, w]
    kern = _proj_kernel
    if colscale is not None:
        kern = _proj_scale_kernel
        in_specs.append(pl.BlockSpec((1, bn), lambda i, j: (0, j)))
        args.append(colscale.reshape(1, n))
    if rope is not None:
        kern = _proj_rope_kernel
        cos, sin = rope
        assert cos.shape == (n // bn, seq, 128)
        per = seq // bm
        tab = pl.BlockSpec((1, bm, 128), lambda i, j: (j, i % per, 0))
        in_specs += [tab, tab]
        args += [cos, sin]
    return pl.pallas_call(
        kern,
        grid=(m // bm, n // bn),
        in_specs=in_specs,
        out_specs=pl.BlockSpec((bm, bn), lambda i, j: (i, j)),
        out_shape=jax.ShapeDtypeStruct((m, n), out_dtype),
        compiler_params=_params("parallel", "arbitrary"),
        name=name,
    )(*args)


def _fill_halo(hp_ref, hc_ref, hn_ref, hext, first, last, bm):
    hext[0:HALO, :] = hp_ref[...]
    hext[HALO:HALO + bm, :] = hc_ref[...]
    hext[HALO + bm:, :] = hn_ref[...]

    @pl.when(first)
    def _():
        hext[0:HALO, :] = jnp.zeros((HALO, hext.shape[1]), hext.dtype)

    @pl.when(last)
    def _():
        hext[HALO + bm:, :] = jnp.zeros((HALO, hext.shape[1]), hext.dtype)


def _conv3(u, cw, cb, bm):
    rows = u.shape[0]
    prev = pltpu.roll(u, 1, axis=0)[HALO:HALO + bm]
    cur = u[HALO:HALO + bm]
    nxt = pltpu.roll(u, rows - 1, axis=0)[HALO:HALO + bm]
    return cb + prev * cw[0:1] + cur * cw[1:2] + nxt * cw[2:3]


def _convproj_silu_kernel(hp_ref, hc_ref, hn_ref, w_ref, cw_ref, cb_ref, s_ref, o_ref, hext,
                          *, bm, per):
    i, j = pl.program_id(0), pl.program_id(1)

    @pl.when(j == 0)
    def _():
        _fill_halo(hp_ref, hc_ref, hn_ref, hext, i % per == 0, i % per == per - 1, bm)

    hx = hext[...]
    for cs in _subtiles(o_ref.shape[1]):
        a = _conv3(_dot(hx, w_ref[:, cs]), cw_ref[:, cs], cb_ref[:, cs], bm)
        o_ref[:, cs] = (_silu(a) * s_ref[:, cs]).astype(o_ref.dtype)


def _convproj_glu_kernel(hp_ref, hc_ref, hn_ref, wa_ref, wg_ref, cwa_ref, cba_ref, cwg_ref, cbg_ref,
                         o_ref, hext, *, bm, per):
    i, j = pl.program_id(0), pl.program_id(1)

    @pl.when(j == 0)
    def _():
        _fill_halo(hp_ref, hc_ref, hn_ref, hext, i % per == 0, i % per == per - 1, bm)

    hx = hext[...]
    a = _conv3(_dot(hx, wa_ref[...]), cwa_ref[...], cba_ref[...], bm)
    g = _conv3(_dot(hx, wg_ref[...]), cwg_ref[...], cbg_ref[...], bm)
    o_ref[...] = (_silu(a) * g).astype(o_ref.dtype)


def _convproj(h, w, cw, cb, seq, colscale=None, glu=False, name="convproj"):
    m, d = h.shape
    n = w.shape[1] // (2 if glu else 1)
    bm = _tile(seq, BM_CONV, HALO)
    bn = _tile(n, BN_CONV, 128)
    per = seq // bm
    nb = bm // HALO
    last_blk = m // HALO - 1
    hspecs = [pl.BlockSpec((HALO, d), lambda i, j: (jnp.maximum(i * nb - 1, 0), 0)),
              pl.BlockSpec((bm, d), lambda i, j: (i, 0)),
              pl.BlockSpec((HALO, d), lambda i, j: (jnp.minimum((i + 1) * nb, last_blk), 0))]
    wspec = lambda off: pl.BlockSpec((d, bn), lambda i, j: (0, j + off))
    cwspec = lambda off: pl.BlockSpec((CONV_W, bn), lambda i, j: (0, j + off))
    cbspec = lambda off: pl.BlockSpec((1, bn), lambda i, j: (0, j + off))
    cb2 = cb.reshape(1, -1)
    if glu:
        off = n // bn
        kern = functools.partial(_convproj_glu_kernel, bm=bm, per=per)
        in_specs = hspecs + [wspec(0), wspec(off), cwspec(0), cbspec(0), cwspec(off), cbspec(off)]
        args = [h, h, h, w, w, cw, cb2, cw, cb2]
    else:
        kern = functools.partial(_convproj_silu_kernel, bm=bm, per=per)
        in_specs = hspecs + [wspec(0), cwspec(0), cbspec(0), cbspec(0)]
        args = [h, h, h, w, cw, cb2, colscale.reshape(1, n)]
    return pl.pallas_call(
        kern,
        grid=(m // bm, n // bn),
        in_specs=in_specs,
        out_specs=pl.BlockSpec((bm, bn), lambda i, j: (i, j)),
        out_shape=jax.ShapeDtypeStruct((m, n), BF16),
        scratch_shapes=[pltpu.VMEM((bm + 2 * HALO, d), BF16)],
        compiler_params=_params("parallel", "arbitrary"),
        name=name,
    )(*args)


def _chunk(c):
    return pl.ds(pl.multiple_of(c * CHUNK, CHUNK), CHUNK)


def _retention_kernel(lg_ref, qc_ref, kc_ref, vc_ref, ql_ref, kl_ref, vl_ref, g_ref,
                      oc_ref, ol_ref, st):
    h = pl.program_id(1)
    L = CHUNK
    dk = qc_ref.shape[-1]
    dv = vc_ref.shape[-1]
    assert dk == L
    nc = qc_ref.shape[1] // L
    nl = ql_ref.shape[1] // L
    lgf, lgb = lg_ref[0, h], lg_ref[1, h]
    row = lax.broadcasted_iota(jnp.int32, (L, L), 0).astype(F32)
    col = lax.broadcasted_iota(jnp.int32, (L, L), 1).astype(F32)
    rel = row - col
    mask = (jnp.where(rel >= 0, jnp.exp(jnp.maximum(rel, 0.0) * lgf), 0.0)
            + jnp.where(rel <= 0, jnp.exp(jnp.maximum(-rel, 0.0) * lgb), 0.0))
    qdec_f = jnp.exp((row + 1.0) * lgf)
    kdec_f = jnp.exp((L - 1.0 - row) * lgf)
    qdec_b = jnp.exp((L - row) * lgb)
    kdec_b = jnp.exp(row * lgb)
    cdec_f = jnp.exp(jnp.full((1, dv), L, F32) * lgf)
    cdec_b = jnp.exp(jnp.full((1, dv), L, F32) * lgb)

    def phase_a(k_ref, v_ref, n, base, carry):
        def body(t, carry):
            Sf, Sb = carry
            cf, cb = t, n - 1 - t
            st[base + cf, 0:dk, :] = Sf.astype(BF16)
            st[base + cb, dk:, :] = Sb.astype(BF16)
            kf = (k_ref[0, _chunk(cf), :].astype(F32) * kdec_f).astype(BF16)
            kb = (k_ref[0, _chunk(cb), :].astype(F32) * kdec_b).astype(BF16)
            Sf = cdec_f * Sf + _dot_tn(kf, v_ref[0, _chunk(cf), :])
            Sb = cdec_b * Sb + _dot_tn(kb, v_ref[0, _chunk(cb), :])
            return Sf, Sb
        return lax.fori_loop(0, n, body, carry, unroll=SCAN_UNROLL)

    def phase_b(q_ref, k_ref, v_ref, o_ref, n, base):
        g = g_ref[...]

        def body(c, carry):
            sl = _chunk(c)
            q, k, v = q_ref[0, sl, :], k_ref[0, sl, :], v_ref[0, sl, :]
            s = (_dot_nt(q, k) * mask).astype(BF16)
            qf = q.astype(F32)
            qd = jnp.concatenate([(qf * qdec_f).astype(BF16), (qf * qdec_b).astype(BF16)], axis=1)
            y = _dot(s, v) + _dot(qd, st[base + c])
            o_ref[0, sl, :] = _rms(y, g).astype(o_ref.dtype)
            return carry
        lax.fori_loop(0, n, body, 0, unroll=SCAN_UNROLL)

    zero = jnp.zeros((dk, dv), F32)
    carry = phase_a(kc_ref, vc_ref, nc, 0, (zero, zero))
    phase_a(kl_ref, vl_ref, nl, nc, carry)
    phase_b(qc_ref, kc_ref, vc_ref, oc_ref, nc, 0)
    phase_b(ql_ref, kl_ref, vl_ref, ol_ref, nl, nc)


def _retention(log_gamma, qk_c, pv_c, qk_l, pv_l, g, v_blk, batch):
    tc, tl = qk_c.shape[1], qk_l.shape[1]
    dk = qk_c.shape[2] // (2 * H_R)
    dv = g.shape[0] // H_R
    qs = lambda t: pl.BlockSpec((1, t, dk), lambda b, h: (b, 0, h))
    ks = lambda t: pl.BlockSpec((1, t, dk), lambda b, h: (b, 0, H_R + h))
    vs = lambda t: pl.BlockSpec((1, t, dv), lambda b, h: (b, 0, v_blk + h))
    os_ = lambda t: pl.BlockSpec((1, t, dv), lambda b, h: (b, 0, h))
    return pl.pallas_call(
        _retention_kernel,
        grid=(batch, H_R),
        in_specs=[pl.BlockSpec(memory_space=pltpu.SMEM),
                  qs(tc), ks(tc), vs(tc), qs(tl), ks(tl), vs(tl),
                  pl.BlockSpec((1, dv), lambda b, h: (0, h))],
        out_specs=[os_(tc), os_(tl)],
        out_shape=[jax.ShapeDtypeStruct((batch, tc, H_R * dv), BF16),
                   jax.ShapeDtypeStruct((batch, tl, H_R * dv), BF16)],
        scratch_shapes=[pltpu.VMEM(((tc + tl) // CHUNK, 2 * dk, dv), BF16)],
        compiler_params=_params("parallel", "arbitrary"),
        name="retention",
    )(log_gamma, qk_c, qk_c, pv_c, qk_l, qk_l, pv_l, g.reshape(1, -1))


def _lane_scan(x, reverse, op, identity):
    n = x.shape[-1]
    lane = lax.broadcasted_iota(jnp.int32, x.shape, 1)
    s = 1
    while s < n:
        if reverse:
            x = op(x, jnp.where(lane < n - s, pltpu.roll(x, n - s, axis=1), identity))
        else:
            x = op(x, jnp.where(lane >= s, pltpu.roll(x, s, axis=1), identity))
        s *= 2
    return x


_B, _R, _CM, _W, _BE, _ML = range(6)


def _col_of_row(r):
    return jnp.broadcast_to(r, (r.shape[1], r.shape[1])).T


def _mlstm_kernel(gb_ref, qc_ref, kc_ref, vc_ref, gic_ref, gfc_ref, gibc_ref, gfbc_ref,
                  ql_ref, kl_ref, vl_ref, gil_ref, gfl_ref, gibl_ref, gfbl_ref, g_ref,
                  oc_ref, ol_ref, cst, mst, kt, sfc, sbc, sfl, sbl):
    h = pl.program_id(1)
    L = CHUNK
    dk = qc_ref.shape[-1]
    dv = vc_ref.shape[-1]
    assert dk == L
    nc = qc_ref.shape[1] // L
    nl = ql_ref.shape[1] // L
    row = lax.broadcasted_iota(jnp.int32, (L, L), 0)
    col = lax.broadcasted_iota(jnp.int32, (L, L), 1)
    tri = (col <= row, col >= row)
    end_lane = (L - 1, 0)

    ones_col = jnp.ones((L, L), BF16)

    def v_aug(v_ref, c):
        return jnp.concatenate([v_ref[0, _chunk(c), :], ones_col], axis=1)

    def prep(gi_ref, gf_ref, scr, d):
        rev = d == 1
        ig = gi_ref[0, 0] + gb_ref[2 * d, h]
        b = _lane_scan(_log_sigmoid(gf_ref[0, 0] + gb_ref[2 * d + 1, h]), rev, jnp.add, 0.0)
        r = ig - b
        lane = lax.broadcasted_iota(jnp.int32, b.shape, 1)
        be = jnp.broadcast_to(
            jnp.sum(jnp.where(lane == end_lane[d], b, 0.0), axis=-1, keepdims=True), b.shape)
        loc = be + r
        m_loc = jnp.broadcast_to(jnp.max(loc, axis=-1, keepdims=True), b.shape)
        scr[_B] = b
        scr[_R] = r
        scr[_CM] = _lane_scan(r, rev, jnp.maximum, -jnp.inf)
        scr[_W] = jnp.exp(loc - m_loc)
        scr[_BE] = be
        scr[_ML] = m_loc

    prep(gic_ref, gfc_ref, sfc, 0)
    prep(gibc_ref, gfbc_ref, sbc, 1)
    prep(gil_ref, gfl_ref, sfl, 0)
    prep(gibl_ref, gfbl_ref, sbl, 1)

    def transpose_k(k_ref, n, base):
        def body(c, carry):
            kt[base + c] = k_ref[0, _chunk(c), :].T
            return carry
        lax.fori_loop(0, n, body, 0, unroll=SCAN_UNROLL)

    transpose_k(kc_ref, nc, 0)
    transpose_k(kl_ref, nl, nc)

    def phase_a(v_ref, scr, n, base, carry):
        def step(d, c, state):
            C, m = state
            s = scr[d]
            cst[d, base + c] = C.astype(BF16)
            mst[d, base + c] = m
            row_of = lambda idx: s[idx, pl.ds(c, 1), :]
            kw = (kt[base + c].astype(F32) * row_of(_W)).astype(BF16)
            dC = _dot(kw, v_aug(v_ref, c))
            be, m_loc = row_of(_BE), row_of(_ML)
            m_new = jnp.maximum(be + m, m_loc)
            f_old = jnp.exp(be + m - m_new)
            f_new = jnp.exp(m_loc - m_new)
            wide = lambda f: jnp.concatenate([f] * (C.shape[1] // L), axis=1)
            return wide(f_old) * C + wide(f_new) * dC, m_new

        def body(t, carry):
            return step(0, t, carry[0]), step(1, n - 1 - t, carry[1])
        return lax.fori_loop(0, n, body, carry, unroll=SCAN_UNROLL)

    def phase_b(q_ref, k_ref, v_ref, o_ref, scr, n, base):
        g = g_ref[...]
        ones_dv = jnp.ones((dv, L), BF16)

        def body(c, carry):
            sl = _chunk(c)
            q, k = q_ref[0, sl, :], k_ref[0, sl, :]
            va = v_aug(v_ref, c)
            qk = _dot_nt(q, k)
            qf = q.astype(F32)
            y = None
            for d in (0, 1):
                s = scr[d]
                row_of = lambda idx: s[idx, pl.ds(c, 1), :]
                m_prev = mst[d, base + c]
                m_t = row_of(_B) + jnp.maximum(row_of(_CM), m_prev)
                U = _col_of_row(row_of(_B) - m_t)
                NM = _col_of_row(-m_t)
                p = jnp.exp(jnp.where(tri[d], U + row_of(_R), -jnp.inf))
                a = jnp.exp(U + m_prev)
                lhs = jnp.concatenate([(qk * p).astype(BF16), (a * qf).astype(BF16)], axis=1)
                rhs = jnp.concatenate([va, cst[d, base + c]], axis=0)
                res = _dot(lhs, rhs)
                inv = 1.0 / jnp.maximum(jnp.abs(res[:, dv:]), jnp.exp(NM))
                hval = res[:, :dv] * jnp.concatenate([inv] * (dv // L), axis=1)
                y = hval if y is None else y + hval
            ms = _dot((y * y).astype(BF16), ones_dv) * (1.0 / dv)
            scale = lax.rsqrt(ms + EPS)
            o_ref[0, sl, :] = (y * jnp.concatenate([scale] * (dv // L), axis=1) * g).astype(o_ref.dtype)
            return carry
        lax.fori_loop(0, n, body, 0, unroll=SCAN_UNROLL)

    zero = (jnp.zeros((dk, dv + L), F32), jnp.zeros((1, L), F32))
    scr_c = (sfc, sbc)
    scr_l = (sfl, sbl)
    carry = phase_a(vc_ref, scr_c, nc, 0, (zero, zero))
    phase_a(vl_ref, scr_l, nl, nc, carry)
    phase_b(qc_ref, kc_ref, vc_ref, oc_ref, scr_c, nc, 0)
    phase_b(ql_ref, kl_ref, vl_ref, ol_ref, scr_l, nl, nc)


def _mlstm(gate_b, qk_c, pv_c, gates_c, qk_l, pv_l, gates_l, g, v_blk, batch):
    tc, tl = qk_c.shape[1], qk_l.shape[1]
    dk = qk_c.shape[2] // (2 * H_M)
    dv = g.shape[0] // H_M
    qs = lambda t: pl.BlockSpec((1, t, dk), lambda b, h: (b, 0, h))
    ks = lambda t: pl.BlockSpec((1, t, dk), lambda b, h: (b, 0, H_M + h))
    vs = lambda t: pl.BlockSpec((1, t, dv), lambda b, h: (b, 0, v_blk + h))
    gs = lambda t, k: pl.BlockSpec((1, 1, t // CHUNK, CHUNK), lambda b, h: (b, k * H_M + h, 0, 0))
    os_ = lambda t: pl.BlockSpec((1, t, dv), lambda b, h: (b, 0, h))

    def stream(t):
        return [qs(t), ks(t), vs(t), gs(t, 0), gs(t, 1), gs(t, 2), gs(t, 3)]

    n_slots = (tc + tl) // CHUNK
    rows = lambda t: pltpu.VMEM((6, t // CHUNK, CHUNK), F32)
    return pl.pallas_call(
        _mlstm_kernel,
        grid=(batch, H_M),
        in_specs=[pl.BlockSpec(memory_space=pltpu.SMEM)] + stream(tc) + stream(tl)
                 + [pl.BlockSpec((1, dv), lambda b, h: (0, h))],
        out_specs=[os_(tc), os_(tl)],
        out_shape=[jax.ShapeDtypeStruct((batch, tc, H_M * dv), BF16),
                   jax.ShapeDtypeStruct((batch, tl, H_M * dv), BF16)],
        scratch_shapes=[pltpu.VMEM((2, n_slots, dk, dv + CHUNK), BF16),
                        pltpu.VMEM((2, n_slots, 1, CHUNK), F32),
                        pltpu.VMEM((n_slots, dk, CHUNK), BF16),
                        rows(tc), rows(tc), rows(tl), rows(tl)],
        compiler_params=_params("parallel", "arbitrary"),
        name="mlstm",
    )(gate_b, qk_c, qk_c, pv_c, gates_c, gates_c, gates_c, gates_c,
      qk_l, qk_l, pv_l, gates_l, gates_l, gates_l, gates_l, g.reshape(1, -1))


def _merge_kernel(yr_ref, rg_ref, ym_ref, mo_ref, gr_ref, gm_ref, wr_ref, wm_ref, o_ref):
    a = (yr_ref[...] * _silu(rg_ref[...].astype(F32)).astype(BF16))
    b = (ym_ref[...] * _sigmoid(mo_ref[...].astype(F32)).astype(BF16))
    y = (_sigmoid(gr_ref[...].astype(F32)) * _dot(a, wr_ref[...])
         + _sigmoid(gm_ref[...].astype(F32)) * _dot(b, wm_ref[...]))
    o_ref[...] = y.astype(o_ref.dtype)


def _merge(yr, ym, pv, wr, wm, blk):
    m, vw = yr.shape
    d = wr.shape[1]
    bm = _tile(m, BM_ROW, 16)
    col = lambda k: pl.BlockSpec((bm, vw), lambda i: (i, k))
    wspec = pl.BlockSpec((vw, d), lambda i: (0, 0), pipeline_mode=pl.Buffered(1))
    return pl.pallas_call(
        _merge_kernel,
        grid=(m // bm,),
        in_specs=[col(0), col(blk[0]), col(0), col(blk[1]), col(blk[2]), col(blk[3]), wspec, wspec],
        out_specs=pl.BlockSpec((bm, d), lambda i: (i, 0)),
        out_shape=jax.ShapeDtypeStruct((m, d), BF16),
        compiler_params=_params("parallel"),
        name="merge",
    )(yr, pv, ym, pv, pv, pv, wr, wm)


def _resid_kernel(a_ref, w_ref, x_ref, go_ref, gate_ref, gn_ref, sh_ref, sc_ref, xo_ref, ho_ref):
    out = _dot(a_ref[...], w_ref[...])
    xn = x_ref[...] + gate_ref[0] * _rms(out, go_ref[...])
    xo_ref[...] = xn
    ho_ref[...] = _normmod(xn, gn_ref[...], sh_ref[0], sc_ref[0]).astype(ho_ref.dtype)


def _resid_last_kernel(a_ref, w_ref, x_ref, go_ref, gate_ref, xo_ref):
    out = _dot(a_ref[...], w_ref[...])
    xo_ref[...] = x_ref[...] + gate_ref[0] * _rms(out, go_ref[...])


def _resid(a, w, x, g_out, gate, rows_per_mod, nxt=None, name="resid"):
    m, k = a.shape
    d = w.shape[1]
    bm = _tile(rows_per_mod, BM_ROW, 16)
    per = rows_per_mod // bm
    row = lambda width: pl.BlockSpec((bm, width), lambda i: (i, 0))
    vec = pl.BlockSpec((1, 1, d), lambda i: (i // per, 0, 0))
    gvec = pl.BlockSpec((1, d), lambda i: (0, 0))
    in_specs = [row(k), pl.BlockSpec((k, d), lambda i: (0, 0), pipeline_mode=pl.Buffered(1)),
                row(d), gvec, vec]
    args = [a, w, x, g_out.reshape(1, d), gate]
    if nxt is None:
        return pl.pallas_call(
            _resid_last_kernel, grid=(m // bm,), in_specs=in_specs, out_specs=row(d),
            out_shape=jax.ShapeDtypeStruct((m, d), F32),
            compiler_params=_params("parallel"), name=name,
        )(*args)
    g_next, shift, scale = nxt
    return pl.pallas_call(
        _resid_kernel, grid=(m // bm,),
        in_specs=in_specs + [gvec, vec, vec],
        out_specs=[row(d), row(d)],
        out_shape=[jax.ShapeDtypeStruct((m, d), F32), jax.ShapeDtypeStruct((m, d), BF16)],
        compiler_params=_params("parallel"), name=name,
    )(*args, g_next.reshape(1, d), shift, scale)


def _rope_tables(seq, dk, q_scale):
    quarter = dk // 4
    t = jnp.arange(seq)
    rows = (t // GRID_W).astype(F32)
    cols = (t % GRID_W).astype(F32)
    freqs = ROPE_BASE ** (-jnp.arange(quarter, dtype=F32) / quarter)
    ar = rows[:, None] * freqs[None, :]
    ac = cols[:, None] * freqs[None, :]
    cos = jnp.concatenate([jnp.cos(ar), jnp.cos(ar), jnp.cos(ac), jnp.cos(ac)], axis=-1)
    sin = jnp.concatenate([-jnp.sin(ar), jnp.sin(ar), -jnp.sin(ac), jnp.sin(ac)], axis=-1)
    scale = jnp.array([q_scale, 1.0], F32)[:, None, None]
    return cos[None] * scale, sin[None] * scale


def _gate_rows(gates, batch, seq):
    g = gates[:, :4 * H_M].reshape(batch, seq, 4 * H_M)
    return jnp.transpose(g, (0, 2, 1)).reshape(batch, 4 * H_M, seq // CHUNK, CHUNK)


def kernel(x, c, ctx, c_ctx, w_ada, b_ada, norm_w, w_in, mlstm_conv_w, mlstm_conv_b, mlstm_gate_b,
           ret_decay_exp, head_norm_w, w_ret_out, w_mlstm_out, w_o, w_up, ffn_conv_w, ffn_conv_b,
           w_down):
    batch, seq, d = x.shape
    ctx_len = ctx.shape[1]
    depth = w_ada.shape[0]
    vw = w_ret_out.shape[1]
    r_qk = (w_in.shape[2] - 4 * vw - 4 * H_M - 2 * d) // 4
    r_dk = r_qk // H_R
    m_dk = r_qk // H_M
    assert d == vw, "column-block addressing of the projection output assumes D_MODEL == VW"

    o_rv = 2 * r_qk
    o_mq = o_rv + 2 * vw
    o_mv = o_mq + 2 * r_qk
    o_mg = o_mv + 2 * vw
    o_merge = o_mg + 4 * H_M

    pad = (-(batch + 1)) % 8
    cvec = jnp.concatenate([c, c_ctx[None], jnp.zeros((pad, d), F32)], axis=0)
    mod = _ada(cvec, w_ada, b_ada)

    def mod_l(l, k):
        return mod[l, :batch, k * d:(k + 1) * d].reshape(batch, 1, d)

    def mod_c(l, k):
        return mod[l, batch:batch + 1, k * d:(k + 1) * d].reshape(1, 1, d)

    rope = _rope_tables(seq, r_dk, r_dk ** -0.5)
    qscale_r = jnp.concatenate([jnp.full((r_qk,), r_dk ** -0.5, F32), jnp.ones((r_qk,), F32)])
    qscale_m = jnp.concatenate([jnp.full((r_qk,), m_dk ** -0.5, F32), jnp.ones((r_qk,), F32)])
    log_gamma = jnp.log1p(-jnp.exp2(-ret_decay_exp.astype(F32)))

    ml, mc = batch * seq, batch * ctx_len
    xl = x.reshape(ml, d)
    xc = ctx.reshape(mc, d)
    hl = _normmod_call(xl, norm_w[0, 0], mod_l(0, 0), mod_l(0, 1), seq)
    hc = _normmod_call(xc, norm_w[0, 0], mod_c(0, 0), mod_c(0, 1), mc)

    for l in range(depth):
        last = l == depth - 1
        wl = w_in[l].astype(BF16)
        w_rqk = wl[:, :o_rv]
        w_mqk = wl[:, o_mq:o_mv]
        w_plain = jnp.concatenate([wl[:, o_rv:o_mq], wl[:, o_mv:o_mg], wl[:, o_merge:]], axis=1)
        w_gate = jnp.pad(wl[:, o_mg:o_merge], ((0, 0), (0, 128 - 4 * H_M)))
        cw, cb = mlstm_conv_w[l], mlstm_conv_b[l]

        def project(h, seq_len, rope_tabs, nm):
            pv = _proj(h, w_plain, name=nm + "_plain")
            gates = _proj(h, w_gate, out_dtype=F32, name=nm + "_gates")
            if rope_tabs is None:
                rqk = _proj(h, w_rqk, colscale=qscale_r, name=nm + "_rqk")
            else:
                rqk = _proj(h, w_rqk, rope=rope_tabs, seq=seq_len, name=nm + "_rqk")
            mqk = _convproj(h, w_mqk, cw, cb, seq_len, colscale=qscale_m, name=nm + "_mqk")
            b = h.shape[0] // seq_len
            return (pv.reshape(b, seq_len, -1), rqk.reshape(b, seq_len, -1),
                    mqk.reshape(b, seq_len, -1), _gate_rows(gates, b, seq_len))

        pv_c, rqk_c, mqk_c, g_c = project(hc, ctx_len, None, "ctx")
        pv_l, rqk_l, mqk_l, g_l = project(hl, seq, rope, "lat")

        yr_c, yr_l = _retention(log_gamma[l], rqk_c, pv_c, rqk_l, pv_l, head_norm_w[l, 0],
                                0 * H_R, batch)
        ym_c, ym_l = _mlstm(mlstm_gate_b[l], mqk_c, pv_c, g_c, mqk_l, pv_l, g_l, head_norm_w[l, 1],
                            2 * H_M, batch)

        wr, wm, wo = (w.astype(BF16) for w in (w_ret_out[l], w_mlstm_out[l], w_o[l]))
        wu, wd = w_up[l].astype(BF16), w_down[l].astype(BF16)
        blk = (1, 3, 4, 5)

        def tail(yr, ym, pv, xres, mod_k, rows_per_mod, seq_len, nm):
            y = _merge(yr.reshape(-1, vw), ym.reshape(-1, vw), pv.reshape(yr.shape[0] * seq_len, -1),
                       wr, wm, blk)
            x1, h2 = _resid(y, wo, xres, norm_w[l, 1], mod_k(l, 2), rows_per_mod,
                            nxt=(norm_w[l, 2], mod_k(l, 3), mod_k(l, 4)), name=nm + "_o")
            act = _convproj(h2, wu, ffn_conv_w[l], ffn_conv_b[l], seq_len, glu=True, name=nm + "_ffn")
            if last:
                return _resid(act, wd, x1, norm_w[l, 3], mod_k(l, 5), rows_per_mod, name=nm + "_down"), None
            return _resid(act, wd, x1, norm_w[l, 3], mod_k(l, 5), rows_per_mod,
                          nxt=(norm_w[l + 1, 0], mod_k(l + 1, 0), mod_k(l + 1, 1)), name=nm + "_down")

        xl, hl = tail(yr_l, ym_l, pv_l, xl, mod_l, seq, seq, "lat")
        if not last:
            xc, hc = tail(yr_c, ym_c, pv_c, xc, mod_c, mc, ctx_len, "ctx")

    return xl.reshape(batch, seq, d)
```

```python
import functools

import jax
import jax.numpy as jnp
from jax import lax
from jax.experimental import pallas as pl
from jax.experimental.pallas import tpu as pltpu

F32 = jnp.float32
BF16 = jnp.bfloat16

GRID_W = 64
CHUNK = 128
H_R = 8
H_M = 8
N_MOD = 6
CONV_W = 3
ROPE_BASE = 10000.0
EPS = 1e-6
HALO = 16
ROWS_PART = 512
V7X_VMEM_LIMIT = 56 * 1024 * 1024

BM_PROJ = 1024
BN_PROJ = 1024
BM_CONV = 1024
BN_CONV = 512
BM_ROW = 256
BM_NORM = 512
BN_ADA = 1024
SCAN_UNROLL = 4


def _tile(n, pref, quantum):
    if n <= pref:
        return n
    t = (pref // quantum) * quantum
    while t > quantum and n % t:
        t -= quantum
    assert n % t == 0, (n, pref, quantum)
    return t


def _params(*sem):
    return pltpu.CompilerParams(dimension_semantics=sem, vmem_limit_bytes=V7X_VMEM_LIMIT)


def _sigmoid(x):
    return 1.0 / (1.0 + jnp.exp(-x))


def _silu(x):
    return x * _sigmoid(x)


def _log_sigmoid(x):
    return jnp.minimum(x, 0.0) - jnp.log1p(jnp.exp(-jnp.abs(x)))


def _rms(x, g):
    ms = jnp.mean(x * x, axis=-1, keepdims=True)
    return x * lax.rsqrt(ms + EPS) * g


def _normmod(x, g, shift, scale):
    return _rms(x, g) * (1.0 + scale) + shift


def _dot(a, b):
    return jnp.dot(a, b, preferred_element_type=F32)


def _dot_nt(a, b):
    return lax.dot_general(a, b, (((1,), (1,)), ((), ())), preferred_element_type=F32)


def _dot_tn(a, b):
    return lax.dot_general(a, b, (((0,), (0,)), ((), ())), preferred_element_type=F32)


def _ada_kernel(c_ref, w_ref, b_ref, o_ref):
    s = _silu(c_ref[...]).astype(BF16)
    o_ref[0] = _dot(s, w_ref[0].astype(BF16)) + b_ref[0]


def _ada(cvec, w_ada, b_ada):
    depth, d, n = w_ada.shape
    r = cvec.shape[0]
    bn = _tile(n, BN_ADA, 128)
    return pl.pallas_call(
        _ada_kernel,
        grid=(depth, n // bn),
        in_specs=[pl.BlockSpec((r, d), lambda l, j: (0, 0)),
                  pl.BlockSpec((1, d, bn), lambda l, j: (l, 0, j)),
                  pl.BlockSpec((1, 1, bn), lambda l, j: (l, 0, j))],
        out_specs=pl.BlockSpec((1, r, bn), lambda l, j: (l, 0, j)),
        out_shape=jax.ShapeDtypeStruct((depth, r, n), F32),
        compiler_params=_params("arbitrary", "arbitrary"),
        name="ada",
    )(cvec, w_ada, b_ada.reshape(depth, 1, n))


def _normmod_kernel(x_ref, g_ref, sh_ref, sc_ref, o_ref):
    o_ref[...] = _normmod(x_ref[...], g_ref[...], sh_ref[0], sc_ref[0]).astype(o_ref.dtype)


def _normmod_call(x, g, shift, scale, rows_per_mod):
    m, d = x.shape
    bm = _tile(rows_per_mod, BM_NORM, 16)
    per = rows_per_mod // bm
    vec = pl.BlockSpec((1, 1, d), lambda i: (i // per, 0, 0))
    return pl.pallas_call(
        _normmod_kernel,
        grid=(m // bm,),
        in_specs=[pl.BlockSpec((bm, d), lambda i: (i, 0)),
                  pl.BlockSpec((1, d), lambda i: (0, 0)), vec, vec],
        out_specs=pl.BlockSpec((bm, d), lambda i: (i, 0)),
        out_shape=jax.ShapeDtypeStruct((m, d), BF16),
        compiler_params=_params("parallel"),
        name="normmod",
    )(x, g.reshape(1, d), shift, scale)


def _proj_kernel(h_ref, w_ref, o_ref):
    o_ref[...] = _dot(h_ref[...], w_ref[...]).astype(o_ref.dtype)


def _proj_scale_kernel(h_ref, w_ref, s_ref, o_ref):
    o_ref[...] = (_dot(h_ref[...], w_ref[...]) * s_ref[...]).astype(o_ref.dtype)


def _swap32(r):
    lane = lax.broadcasted_iota(jnp.int32, r.shape, 1)
    return jnp.where((lane & 32) == 0, pltpu.roll(r, 96, axis=1), pltpu.roll(r, 32, axis=1))


def _row_parts(bm, rows):
    rb = rows if bm % rows == 0 else bm
    return [(r, rb) for r in range(0, bm, rb)]


def _proj_rope_kernel(h_ref, w_ref, c_ref, s_ref, o_ref):
    for r0, rb in _row_parts(h_ref.shape[0], ROWS_PART):
        rs = slice(r0, r0 + rb)
        res = _dot(h_ref[rs, :], w_ref[...])
        cos, sin = c_ref[0, rs, :], s_ref[0, rs, :]
        for g in range(res.shape[1] // 128):
            r = res[:, g * 128:(g + 1) * 128]
            o_ref[rs, g * 128:(g + 1) * 128] = (r * cos + _swap32(r) * sin).astype(o_ref.dtype)


def _col_block_map(ranges, bn):
    assert all(first % bn == 0 and cnt % bn == 0 for first, cnt in ranges), (ranges, bn)

    def block(j):
        blk = j + ranges[0][0] // bn
        seen = 0
        for (first, cnt), (nxt, _) in zip(ranges[:-1], ranges[1:]):
            seen += cnt // bn
            blk = blk + jnp.where(j >= seen, (nxt - first - cnt) // bn, 0)
        return blk
    return block


def _proj(h, w, layer, ranges, out_dtype=None, colscale=None, rope=None, seq=None, name="proj"):
    out_dtype = out_dtype or BF16
    m, d = h.shape
    n = sum(cnt for _, cnt in ranges)
    bm = _tile(m if seq is None else seq, BM_PROJ, 16)
    bn = _tile(n, BN_PROJ, 128) if rope is None else n // 2
    block = _col_block_map(ranges, bn)
    in_specs = [pl.BlockSpec((bm, d), lambda i, j: (i, 0)),
                pl.BlockSpec((None, d, bn), lambda i, j: (layer, 0, block(j)))]
    args = [h, w]
    kern = _proj_kernel
    if colscale is not None:
        kern = _proj_scale_kernel
        in_specs.append(pl.BlockSpec((1, bn), lambda i, j: (0, j)))
        args.append(colscale.reshape(1, n))
    if rope is not None:
        kern = _proj_rope_kernel
        cos, sin = rope
        assert cos.shape == (n // bn, seq, 128)
        per = seq // bm
        tab = pl.BlockSpec((1, bm, 128), lambda i, j: (j, i % per, 0))
        in_specs += [tab, tab]
        args += [cos, sin]
    return pl.pallas_call(
        kern,
        grid=(m // bm, n // bn),
        in_specs=in_specs,
        out_specs=pl.BlockSpec((bm, bn), lambda i, j: (i, j)),
        out_shape=jax.ShapeDtypeStruct((m, n), out_dtype),
        compiler_params=_params("parallel", "arbitrary"),
        name=name,
    )(*args)


def _fill_halo(hp_ref, hc_ref, hn_ref, hext, first, last, bm):
    hext[0:HALO, :] = hp_ref[...]
    hext[HALO:HALO + bm, :] = hc_ref[...]
    hext[HALO + bm:, :] = hn_ref[...]

    @pl.when(first)
    def _():
        hext[0:HALO, :] = jnp.zeros((HALO, hext.shape[1]), hext.dtype)

    @pl.when(last)
    def _():
        hext[HALO + bm:, :] = jnp.zeros((HALO, hext.shape[1]), hext.dtype)


def _conv3(u, cw, cb, bm):
    rows = u.shape[0]
    prev = pltpu.roll(u, 1, axis=0)[HALO:HALO + bm]
    cur = u[HALO:HALO + bm]
    nxt = pltpu.roll(u, rows - 1, axis=0)[HALO:HALO + bm]
    return cb + prev * cw[0:1] + cur * cw[1:2] + nxt * cw[2:3]


def _convproj_silu_kernel(hp_ref, hc_ref, hn_ref, w_ref, cw_ref, cb_ref, s_ref, o_ref, hext,
                          *, bm, per):
    i, j = pl.program_id(0), pl.program_id(1)

    @pl.when(j == 0)
    def _():
        _fill_halo(hp_ref, hc_ref, hn_ref, hext, i % per == 0, i % per == per - 1, bm)

    for r, rb in _row_parts(bm, ROWS_PART):
        u = _dot(hext[r:r + rb + 2 * HALO, :], w_ref[...])
        a = _conv3(u, cw_ref[...], cb_ref[...], rb)
        o_ref[r:r + rb, :] = (_silu(a) * s_ref[...]).astype(o_ref.dtype)


def _convproj_glu_kernel(hp_ref, hc_ref, hn_ref, wa_ref, wg_ref, cwa_ref, cba_ref, cwg_ref, cbg_ref,
                         o_ref, hext, *, bm, per):
    i, j = pl.program_id(0), pl.program_id(1)

    @pl.when(j == 0)
    def _():
        _fill_halo(hp_ref, hc_ref, hn_ref, hext, i % per == 0, i % per == per - 1, bm)

    hx = hext[...]
    a = _conv3(_dot(hx, wa_ref[...]), cwa_ref[...], cba_ref[...], bm)
    g = _conv3(_dot(hx, wg_ref[...]), cwg_ref[...], cbg_ref[...], bm)
    o_ref[...] = (_silu(a) * g).astype(o_ref.dtype)


def _convproj(h, w, layer, col0, n, cw, cb, seq, colscale=None, glu=False, name="convproj"):
    m, d = h.shape
    bm = _tile(seq, BM_CONV, HALO)
    bn = _tile(n, BN_CONV, 128)
    assert col0 % bn == 0
    first = col0 // bn
    per = seq // bm
    nb = bm // HALO
    last_blk = m // HALO - 1
    hspecs = [pl.BlockSpec((HALO, d), lambda i, j: (jnp.maximum(i * nb - 1, 0), 0)),
              pl.BlockSpec((bm, d), lambda i, j: (i, 0)),
              pl.BlockSpec((HALO, d), lambda i, j: (jnp.minimum((i + 1) * nb, last_blk), 0))]
    wspec = lambda off: pl.BlockSpec((None, d, bn), lambda i, j: (layer, 0, first + j + off))
    cwspec = lambda off: pl.BlockSpec((CONV_W, bn), lambda i, j: (0, j + off))
    cbspec = lambda off: pl.BlockSpec((1, bn), lambda i, j: (0, j + off))
    cb2 = cb.reshape(1, -1)
    if glu:
        off = n // bn
        kern = functools.partial(_convproj_glu_kernel, bm=bm, per=per)
        in_specs = hspecs + [wspec(0), wspec(off), cwspec(0), cbspec(0), cwspec(off), cbspec(off)]
        args = [h, h, h, w, w, cw, cb2, cw, cb2]
    else:
        kern = functools.partial(_convproj_silu_kernel, bm=bm, per=per)
        in_specs = hspecs + [wspec(0), cwspec(0), cbspec(0), cbspec(0)]
        args = [h, h, h, w, cw, cb2, colscale.reshape(1, n)]
    return pl.pallas_call(
        kern,
        grid=(m // bm, n // bn),
        in_specs=in_specs,
        out_specs=pl.BlockSpec((bm, bn), lambda i, j: (i, j)),
        out_shape=jax.ShapeDtypeStruct((m, n), BF16),
        scratch_shapes=[pltpu.VMEM((bm + 2 * HALO, d), BF16)],
        compiler_params=_params("parallel", "arbitrary"),
        name=name,
    )(*args)


def _chunk(c):
    return pl.ds(pl.multiple_of(c * CHUNK, CHUNK), CHUNK)


def _retention_kernel(lg_ref, qc_ref, kc_ref, vc_ref, ql_ref, kl_ref, vl_ref, g_ref,
                      oc_ref, ol_ref, st):
    h = pl.program_id(1)
    L = CHUNK
    dk = qc_ref.shape[-1]
    dv = vc_ref.shape[-1]
    assert dk == L
    nc = qc_ref.shape[1] // L
    nl = ql_ref.shape[1] // L
    lgf, lgb = lg_ref[0, h], lg_ref[1, h]
    row = lax.broadcasted_iota(jnp.int32, (L, L), 0).astype(F32)
    col = lax.broadcasted_iota(jnp.int32, (L, L), 1).astype(F32)
    rel = row - col
    mask = (jnp.where(rel >= 0, jnp.exp(jnp.maximum(rel, 0.0) * lgf), 0.0)
            + jnp.where(rel <= 0, jnp.exp(jnp.maximum(-rel, 0.0) * lgb), 0.0))
    qdec_f = jnp.exp((row + 1.0) * lgf)
    kdec_f = jnp.exp((L - 1.0 - row) * lgf)
    qdec_b = jnp.exp((L - row) * lgb)
    kdec_b = jnp.exp(row * lgb)
    cdec_f = jnp.exp(jnp.full((1, dv), L, F32) * lgf)
    cdec_b = jnp.exp(jnp.full((1, dv), L, F32) * lgb)

    def phase_a(k_ref, v_ref, n, base, carry):
        def body(t, carry):
            Sf, Sb = carry
            cf, cb = t, n - 1 - t
            st[base + cf, 0:dk, :] = Sf.astype(BF16)
            st[base + cb, dk:, :] = Sb.astype(BF16)
            kf = (k_ref[0, _chunk(cf), :].astype(F32) * kdec_f).astype(BF16)
            kb = (k_ref[0, _chunk(cb), :].astype(F32) * kdec_b).astype(BF16)
            Sf = cdec_f * Sf + _dot_tn(kf, v_ref[0, _chunk(cf), :])
            Sb = cdec_b * Sb + _dot_tn(kb, v_ref[0, _chunk(cb), :])
            return Sf, Sb
        return lax.fori_loop(0, n, body, carry, unroll=SCAN_UNROLL)

    def phase_b(q_ref, k_ref, v_ref, o_ref, n, base):
        g = g_ref[...]

        def body(c, carry):
            sl = _chunk(c)
            q, k, v = q_ref[0, sl, :], k_ref[0, sl, :], v_ref[0, sl, :]
            s = (_dot_nt(q, k) * mask).astype(BF16)
            qf = q.astype(F32)
            qd = jnp.concatenate([(qf * qdec_f).astype(BF16), (qf * qdec_b).astype(BF16)], axis=1)
            y = _dot(s, v) + _dot(qd, st[base + c])
            o_ref[0, sl, :] = _rms(y, g).astype(o_ref.dtype)
            return carry
        lax.fori_loop(0, n, body, 0, unroll=SCAN_UNROLL)

    zero = jnp.zeros((dk, dv), F32)
    carry = phase_a(kc_ref, vc_ref, nc, 0, (zero, zero))
    phase_a(kl_ref, vl_ref, nl, nc, carry)
    phase_b(qc_ref, kc_ref, vc_ref, oc_ref, nc, 0)
    phase_b(ql_ref, kl_ref, vl_ref, ol_ref, nl, nc)


def _retention(log_gamma, qk_c, pv_c, qk_l, pv_l, g, v_blk, batch):
    tc, tl = qk_c.shape[1], qk_l.shape[1]
    dk = qk_c.shape[2] // (2 * H_R)
    dv = g.shape[0] // H_R
    qs = lambda t: pl.BlockSpec((1, t, dk), lambda b, h: (b, 0, h))
    ks = lambda t: pl.BlockSpec((1, t, dk), lambda b, h: (b, 0, H_R + h))
    vs = lambda t: pl.BlockSpec((1, t, dv), lambda b, h: (b, 0, v_blk + h))
    os_ = lambda t: pl.BlockSpec((1, t, dv), lambda b, h: (b, 0, h))
    return pl.pallas_call(
        _retention_kernel,
        grid=(batch, H_R),
        in_specs=[pl.BlockSpec(memory_space=pltpu.SMEM),
                  qs(tc), ks(tc), vs(tc), qs(tl), ks(tl), vs(tl),
                  pl.BlockSpec((1, dv), lambda b, h: (0, h))],
        out_specs=[os_(tc), os_(tl)],
        out_shape=[jax.ShapeDtypeStruct((batch, tc, H_R * dv), BF16),
                   jax.ShapeDtypeStruct((batch, tl, H_R * dv), BF16)],
        scratch_shapes=[pltpu.VMEM(((tc + tl) // CHUNK, 2 * dk, dv), BF16)],
        compiler_params=_params("parallel", "arbitrary"),
        name="retention",
    )(log_gamma, qk_c, qk_c, pv_c, qk_l, qk_l, pv_l, g.reshape(1, -1))


def _lane_scan(x, reverse, op, identity):
    n = x.shape[-1]
    lane = lax.broadcasted_iota(jnp.int32, x.shape, 1)
    s = 1
    while s < n:
        if reverse:
            x = op(x, jnp.where(lane < n - s, pltpu.roll(x, n - s, axis=1), identity))
        else:
            x = op(x, jnp.where(lane >= s, pltpu.roll(x, s, axis=1), identity))
        s *= 2
    return x


_B, _R, _CM, _W, _BE, _ML = range(6)


def _col_of_row(r):
    return jnp.broadcast_to(r, (r.shape[1], r.shape[1])).T


def _mlstm_kernel(gb_ref, qc_ref, kc_ref, vc_ref, gic_ref, gfc_ref, gibc_ref, gfbc_ref,
                  ql_ref, kl_ref, vl_ref, gil_ref, gfl_ref, gibl_ref, gfbl_ref, g_ref,
                  oc_ref, ol_ref, cst, mst, kt, sfc, sbc, sfl, sbl):
    h = pl.program_id(1)
    L = CHUNK
    dk = qc_ref.shape[-1]
    dv = vc_ref.shape[-1]
    assert dk == L
    nc = qc_ref.shape[1] // L
    nl = ql_ref.shape[1] // L
    row = lax.broadcasted_iota(jnp.int32, (L, L), 0)
    col = lax.broadcasted_iota(jnp.int32, (L, L), 1)
    tri = (col <= row, col >= row)
    end_lane = (L - 1, 0)

    ones_col = jnp.ones((L, L), BF16)

    def v_aug(v_ref, c):
        return jnp.concatenate([v_ref[0, _chunk(c), :], ones_col], axis=1)

    def prep(gi_ref, gf_ref, scr, d):
        rev = d == 1
        ig = gi_ref[0, 0] + gb_ref[2 * d, h]
        b = _lane_scan(_log_sigmoid(gf_ref[0, 0] + gb_ref[2 * d + 1, h]), rev, jnp.add, 0.0)
        r = ig - b
        lane = lax.broadcasted_iota(jnp.int32, b.shape, 1)
        be = jnp.broadcast_to(
            jnp.sum(jnp.where(lane == end_lane[d], b, 0.0), axis=-1, keepdims=True), b.shape)
        loc = be + r
        m_loc = jnp.broadcast_to(jnp.max(loc, axis=-1, keepdims=True), b.shape)
        scr[_B] = b
        scr[_R] = r
        scr[_CM] = _lane_scan(r, rev, jnp.maximum, -jnp.inf)
        scr[_W] = jnp.exp(loc - m_loc)
        scr[_BE] = be
        scr[_ML] = m_loc

    prep(gic_ref, gfc_ref, sfc, 0)
    prep(gibc_ref, gfbc_ref, sbc, 1)
    prep(gil_ref, gfl_ref, sfl, 0)
    prep(gibl_ref, gfbl_ref, sbl, 1)

    def transpose_k(k_ref, n, base):
        def body(c, carry):
            kt[base + c] = k_ref[0, _chunk(c), :].T
            return carry
        lax.fori_loop(0, n, body, 0, unroll=SCAN_UNROLL)

    transpose_k(kc_ref, nc, 0)
    transpose_k(kl_ref, nl, nc)

    def phase_a(v_ref, scr, n, base, carry):
        def step(d, c, state):
            C, m = state
            s = scr[d]
            cst[d, base + c] = C.astype(BF16)
            mst[d, base + c] = m
            row_of = lambda idx: s[idx, pl.ds(c, 1), :]
            kw = (kt[base + c].astype(F32) * row_of(_W)).astype(BF16)
            dC = _dot(kw, v_aug(v_ref, c))
            be, m_loc = row_of(_BE), row_of(_ML)
            m_new = jnp.maximum(be + m, m_loc)
            f_old = jnp.exp(be + m - m_new)
            f_new = jnp.exp(m_loc - m_new)
            wide = lambda f: jnp.concatenate([f] * (C.shape[1] // L), axis=1)
            return wide(f_old) * C + wide(f_new) * dC, m_new

        def body(t, carry):
            return step(0, t, carry[0]), step(1, n - 1 - t, carry[1])
        return lax.fori_loop(0, n, body, carry, unroll=SCAN_UNROLL)

    def phase_b(q_ref, k_ref, v_ref, o_ref, scr, n, base):
        g = g_ref[...]
        ones_dv = jnp.ones((dv, L), BF16)

        def body(c, carry):
            sl = _chunk(c)
            q, k = q_ref[0, sl, :], k_ref[0, sl, :]
            va = v_aug(v_ref, c)
            qk = _dot_nt(q, k)
            qf = q.astype(F32)
            y = None
            for d in (0, 1):
                s = scr[d]
                row_of = lambda idx: s[idx, pl.ds(c, 1), :]
                m_prev = mst[d, base + c]
                m_t = row_of(_B) + jnp.maximum(row_of(_CM), m_prev)
                U = _col_of_row(row_of(_B) - m_t)
                NM = _col_of_row(-m_t)
                p = jnp.exp(jnp.where(tri[d], U + row_of(_R), -jnp.inf))
                a = jnp.exp(U + m_prev)
                lhs = jnp.concatenate([(qk * p).astype(BF16), (a * qf).astype(BF16)], axis=1)
                rhs = jnp.concatenate([va, cst[d, base + c]], axis=0)
                res = _dot(lhs, rhs)
                inv = 1.0 / jnp.maximum(jnp.abs(res[:, dv:]), jnp.exp(NM))
                hval = res[:, :dv] * jnp.concatenate([inv] * (dv // L), axis=1)
                y = hval if y is None else y + hval
            ms = _dot((y * y).astype(BF16), ones_dv) * (1.0 / dv)
            scale = lax.rsqrt(ms + EPS)
            o_ref[0, sl, :] = (y * jnp.concatenate([scale] * (dv // L), axis=1) * g).astype(o_ref.dtype)
            return carry
        lax.fori_loop(0, n, body, 0, unroll=SCAN_UNROLL)

    zero = (jnp.zeros((dk, dv + L), F32), jnp.zeros((1, L), F32))
    scr_c = (sfc, sbc)
    scr_l = (sfl, sbl)
    carry = phase_a(vc_ref, scr_c, nc, 0, (zero, zero))
    phase_a(vl_ref, scr_l, nl, nc, carry)
    phase_b(qc_ref, kc_ref, vc_ref, oc_ref, scr_c, nc, 0)
    phase_b(ql_ref, kl_ref, vl_ref, ol_ref, scr_l, nl, nc)


def _mlstm(gate_b, qk_c, pv_c, gates_c, qk_l, pv_l, gates_l, g, v_blk, batch):
    tc, tl = qk_c.shape[1], qk_l.shape[1]
    dk = qk_c.shape[2] // (2 * H_M)
    dv = g.shape[0] // H_M
    qs = lambda t: pl.BlockSpec((1, t, dk), lambda b, h: (b, 0, h))
    ks = lambda t: pl.BlockSpec((1, t, dk), lambda b, h: (b, 0, H_M + h))
    vs = lambda t: pl.BlockSpec((1, t, dv), lambda b, h: (b, 0, v_blk + h))
    gs = lambda t, k: pl.BlockSpec((1, 1, t // CHUNK, CHUNK), lambda b, h: (b, k * H_M + h, 0, 0))
    os_ = lambda t: pl.BlockSpec((1, t, dv), lambda b, h: (b, 0, h))

    def stream(t):
        return [qs(t), ks(t), vs(t), gs(t, 0), gs(t, 1), gs(t, 2), gs(t, 3)]

    n_slots = (tc + tl) // CHUNK
    rows = lambda t: pltpu.VMEM((6, t // CHUNK, CHUNK), F32)
    return pl.pallas_call(
        _mlstm_kernel,
        grid=(batch, H_M),
        in_specs=[pl.BlockSpec(memory_space=pltpu.SMEM)] + stream(tc) + stream(tl)
                 + [pl.BlockSpec((1, dv), lambda b, h: (0, h))],
        out_specs=[os_(tc), os_(tl)],
        out_shape=[jax.ShapeDtypeStruct((batch, tc, H_M * dv), BF16),
                   jax.ShapeDtypeStruct((batch, tl, H_M * dv), BF16)],
        scratch_shapes=[pltpu.VMEM((2, n_slots, dk, dv + CHUNK), BF16),
                        pltpu.VMEM((2, n_slots, 1, CHUNK), F32),
                        pltpu.VMEM((n_slots, dk, CHUNK), BF16),
                        rows(tc), rows(tc), rows(tl), rows(tl)],
        compiler_params=_params("parallel", "arbitrary"),
        name="mlstm",
    )(gate_b, qk_c, qk_c, pv_c, gates_c, gates_c, gates_c, gates_c,
      qk_l, qk_l, pv_l, gates_l, gates_l, gates_l, gates_l, g.reshape(1, -1))


def _merge_kernel(yr_ref, rg_ref, ym_ref, mo_ref, gr_ref, gm_ref, wr_ref, wm_ref, o_ref):
    a = (yr_ref[...] * _silu(rg_ref[...].astype(F32)).astype(BF16))
    b = (ym_ref[...] * _sigmoid(mo_ref[...].astype(F32)).astype(BF16))
    y = (_sigmoid(gr_ref[...].astype(F32)) * _dot(a, wr_ref[...])
         + _sigmoid(gm_ref[...].astype(F32)) * _dot(b, wm_ref[...]))
    o_ref[...] = y.astype(o_ref.dtype)


def _merge(yr, ym, pa, pb, wr, wm, layer):
    m, vw = yr.shape
    d = wr.shape[2]
    bm = _tile(m, BM_ROW, 16)
    col = lambda k: pl.BlockSpec((bm, vw), lambda i: (i, k))
    wspec = pl.BlockSpec((None, vw, d), lambda i: (layer, 0, 0), pipeline_mode=pl.Buffered(1))
    return pl.pallas_call(
        _merge_kernel,
        grid=(m // bm,),
        in_specs=[col(0), col(1), col(0), col(3), col(0), col(1), wspec, wspec],
        out_specs=pl.BlockSpec((bm, d), lambda i: (i, 0)),
        out_shape=jax.ShapeDtypeStruct((m, d), BF16),
        compiler_params=_params("parallel"),
        name="merge",
    )(yr, pa, ym, pa, pb, pb, wr, wm)


def _resid_kernel(a_ref, w_ref, x_ref, go_ref, gate_ref, gn_ref, sh_ref, sc_ref, xo_ref, ho_ref):
    for r, rb in _row_parts(a_ref.shape[0], BM_ROW):
        rs = slice(r, r + rb)
        out = _dot(a_ref[rs, :], w_ref[...])
        xn = x_ref[rs, :] + gate_ref[0] * _rms(out, go_ref[...])
        xo_ref[rs, :] = xn
        ho_ref[rs, :] = _normmod(xn, gn_ref[...], sh_ref[0], sc_ref[0]).astype(ho_ref.dtype)


def _resid_last_kernel(a_ref, w_ref, x_ref, go_ref, gate_ref, xo_ref):
    for r, rb in _row_parts(a_ref.shape[0], BM_ROW):
        rs = slice(r, r + rb)
        out = _dot(a_ref[rs, :], w_ref[...])
        xo_ref[rs, :] = x_ref[rs, :] + gate_ref[0] * _rms(out, go_ref[...])


def _resid(a, w, layer, x, g_out, gate, rows_per_mod, nxt=None, name="resid"):
    m, k = a.shape
    d = w.shape[2]
    bm = _tile(rows_per_mod, 2 * BM_ROW, 16)
    tiles = 2 * bm * (2 * k + 4 * d + 4 * d + 2 * d) + bm * 4 * d
    if 2 * k * d + tiles > V7X_VMEM_LIMIT:
        bm = _tile(rows_per_mod, BM_ROW, 16)
    per = rows_per_mod // bm
    row = lambda width: pl.BlockSpec((bm, width), lambda i: (i, 0))
    vec = pl.BlockSpec((1, 1, d), lambda i: (i // per, 0, 0))
    gvec = pl.BlockSpec((1, d), lambda i: (0, 0))
    in_specs = [row(k),
                pl.BlockSpec((None, k, d), lambda i: (layer, 0, 0), pipeline_mode=pl.Buffered(1)),
                row(d), gvec, vec]
    args = [a, w, x, g_out.reshape(1, d), gate]
    if nxt is None:
        return pl.pallas_call(
            _resid_last_kernel, grid=(m // bm,), in_specs=in_specs, out_specs=row(d),
            out_shape=jax.ShapeDtypeStruct((m, d), F32),
            compiler_params=_params("parallel"), name=name,
        )(*args)
    g_next, shift, scale = nxt
    return pl.pallas_call(
        _resid_kernel, grid=(m // bm,),
        in_specs=in_specs + [gvec, vec, vec],
        out_specs=[row(d), row(d)],
        out_shape=[jax.ShapeDtypeStruct((m, d), F32), jax.ShapeDtypeStruct((m, d), BF16)],
        compiler_params=_params("parallel"), name=name,
    )(*args, g_next.reshape(1, d), shift, scale)


def _rope_tables(seq, dk, q_scale):
    quarter = dk // 4
    t = jnp.arange(seq)
    rows = (t // GRID_W).astype(F32)
    cols = (t % GRID_W).astype(F32)
    freqs = ROPE_BASE ** (-jnp.arange(quarter, dtype=F32) / quarter)
    ar = rows[:, None] * freqs[None, :]
    ac = cols[:, None] * freqs[None, :]
    cos = jnp.concatenate([jnp.cos(ar), jnp.cos(ar), jnp.cos(ac), jnp.cos(ac)], axis=-1)
    sin = jnp.concatenate([-jnp.sin(ar), jnp.sin(ar), -jnp.sin(ac), jnp.sin(ac)], axis=-1)
    scale = jnp.array([q_scale, 1.0], F32)[:, None, None]
    return cos[None] * scale, sin[None] * scale


def _gate_rows(gates, batch, seq):
    g = gates[:, :4 * H_M].reshape(batch, seq, 4 * H_M)
    return jnp.transpose(g, (0, 2, 1)).reshape(batch, 4 * H_M, seq // CHUNK, CHUNK)


def kernel(x, c, ctx, c_ctx, w_ada, b_ada, norm_w, w_in, mlstm_conv_w, mlstm_conv_b, mlstm_gate_b,
           ret_decay_exp, head_norm_w, w_ret_out, w_mlstm_out, w_o, w_up, ffn_conv_w, ffn_conv_b,
           w_down):
    batch, seq, d = x.shape
    ctx_len = ctx.shape[1]
    depth = w_ada.shape[0]
    vw = w_ret_out.shape[1]
    r_qk = (w_in.shape[2] - 4 * vw - 4 * H_M - 2 * d) // 4
    r_dk = r_qk // H_R
    m_dk = r_qk // H_M
    assert d == vw, "column-block addressing of the projection output assumes D_MODEL == VW"

    o_rv = 2 * r_qk
    o_mq = o_rv + 2 * vw
    o_mv = o_mq + 2 * r_qk
    o_mg = o_mv + 2 * vw
    o_merge = o_mg + 4 * H_M

    pad = (-(batch + 1)) % 8
    cvec = jnp.concatenate([c, c_ctx[None], jnp.zeros((pad, d), F32)], axis=0)
    mod = _ada(cvec, w_ada, b_ada)

    def mod_l(l, k):
        return mod[l, :batch, k * d:(k + 1) * d].reshape(batch, 1, d)

    def mod_c(l, k):
        return mod[l, batch:batch + 1, k * d:(k + 1) * d].reshape(1, 1, d)

    rope = _rope_tables(seq, r_dk, r_dk ** -0.5)
    qscale_r = jnp.concatenate([jnp.full((r_qk,), r_dk ** -0.5, F32), jnp.ones((r_qk,), F32)])
    qscale_m = jnp.concatenate([jnp.full((r_qk,), m_dk ** -0.5, F32), jnp.ones((r_qk,), F32)])
    log_gamma = jnp.log1p(-jnp.exp2(-ret_decay_exp.astype(F32)))

    ml, mc = batch * seq, batch * ctx_len
    xl = x.reshape(ml, d)
    xc = ctx.reshape(mc, d)
    hl = _normmod_call(xl, norm_w[0, 0], mod_l(0, 0), mod_l(0, 1), seq)
    hc = _normmod_call(xc, norm_w[0, 0], mod_c(0, 0), mod_c(0, 1), mc)

    w_in_b = w_in.astype(BF16)
    w_mg_b = w_in_b[:, :, o_merge:]
    w_gate_b = jnp.pad(w_in_b[:, :, o_mg:o_merge], ((0, 0), (0, 0), (0, 128 - 4 * H_M)))
    wr_b, wm_b, wo_b, wu_b, wd_b = (w.astype(BF16) for w in (w_ret_out, w_mlstm_out, w_o, w_up, w_down))
    d_ff = w_down.shape[1]

    for l in range(depth):
        last = l == depth - 1
        cw, cb = mlstm_conv_w[l], mlstm_conv_b[l]

        def project(h, seq_len, rope_tabs, nm):
            pa = _proj(h, w_in_b, l, ((o_rv, 2 * vw), (o_mv, 2 * vw)), name=nm + "_plain")
            pb = _proj(h, w_mg_b, l, ((0, 2 * d),), name=nm + "_mgate")
            gates = _proj(h, w_gate_b, l, ((0, 128),), out_dtype=F32, name=nm + "_gates")
            if rope_tabs is None:
                rqk = _proj(h, w_in_b, l, ((0, o_rv),), colscale=qscale_r, name=nm + "_rqk")
            else:
                rqk = _proj(h, w_in_b, l, ((0, o_rv),), rope=rope_tabs, seq=seq_len, name=nm + "_rqk")
            mqk = _convproj(h, w_in_b, l, o_mq, 2 * r_qk, cw, cb, seq_len, colscale=qscale_m,
                            name=nm + "_mqk")
            b = h.shape[0] // seq_len
            return (pa, pb, rqk.reshape(b, seq_len, -1), mqk.reshape(b, seq_len, -1),
                    _gate_rows(gates, b, seq_len))

        pa_c, pb_c, rqk_c, mqk_c, g_c = project(hc, ctx_len, None, "ctx")
        pa_l, pb_l, rqk_l, mqk_l, g_l = project(hl, seq, rope, "lat")
        pv_c = pa_c.reshape(batch, ctx_len, -1)
        pv_l = pa_l.reshape(batch, seq, -1)

        yr_c, yr_l = _retention(log_gamma[l], rqk_c, pv_c, rqk_l, pv_l, head_norm_w[l, 0],
                                0 * H_R, batch)
        ym_c, ym_l = _mlstm(mlstm_gate_b[l], mqk_c, pv_c, g_c, mqk_l, pv_l, g_l, head_norm_w[l, 1],
                            2 * H_M, batch)

        def tail(yr, ym, pa, pb, xres, mod_k, rows_per_mod, seq_len, nm):
            y = _merge(yr.reshape(-1, vw), ym.reshape(-1, vw), pa, pb, wr_b, wm_b, l)
            x1, h2 = _resid(y, wo_b, l, xres, norm_w[l, 1], mod_k(l, 2), rows_per_mod,
                            nxt=(norm_w[l, 2], mod_k(l, 3), mod_k(l, 4)), name=nm + "_o")
            act = _convproj(h2, wu_b, l, 0, d_ff, ffn_conv_w[l], ffn_conv_b[l], seq_len, glu=True,
                            name=nm + "_ffn")
            if last:
                return _resid(act, wd_b, l, x1, norm_w[l, 3], mod_k(l, 5), rows_per_mod,
                              name=nm + "_down"), None
            return _resid(act, wd_b, l, x1, norm_w[l, 3], mod_k(l, 5), rows_per_mod,
                          nxt=(norm_w[l + 1, 0], mod_k(l + 1, 0), mod_k(l + 1, 1)), name=nm + "_down")

        xl, hl = tail(yr_l, ym_l, pa_l, pb_l, xl, mod_l, seq, seq, "lat")
        if not last:
            xc, hc = tail(yr_c, ym_c, pa_c, pb_c, xc, mod_c, mc, ctx_len, "ctx")

    return xl.reshape(batch, seq, d)
```

```python
import functools

import jax
import jax.numpy as jnp
from jax import lax
from jax.experimental import pallas as pl
from jax.experimental.pallas import tpu as pltpu

F32 = jnp.float32
BF16 = jnp.bfloat16

GRID_W = 64
CHUNK = 128
H_R = 8
H_M = 8
N_MOD = 6
CONV_W = 3
ROPE_BASE = 10000.0
EPS = 1e-6
LOG2E = 1.4426950408889634
HALO = 16
ROWS_PART = 512
V7X_VMEM_LIMIT = 56 * 1024 * 1024

BM_PROJ = 1024
BN_PROJ = 1024
BM_CONV = 1024
BN_CONV = 512
BM_ROW = 256
BM_NORM = 512
BN_ADA = 1024
SCAN_UNROLL = 4


def _tile(n, pref, quantum):
    if n <= pref:
        return n
    t = (pref // quantum) * quantum
    while t > quantum and n % t:
        t -= quantum
    assert n % t == 0, (n, pref, quantum)
    return t


def _params(*sem):
    return pltpu.CompilerParams(dimension_semantics=sem, vmem_limit_bytes=V7X_VMEM_LIMIT)


def _sigmoid(x):
    return 1.0 / (1.0 + jnp.exp(-x))


def _silu(x):
    return x * _sigmoid(x)


def _log_sigmoid(x):
    return jnp.minimum(x, 0.0) - jnp.log1p(jnp.exp(-jnp.abs(x)))


def _rms(x, g):
    ms = jnp.mean(x * x, axis=-1, keepdims=True)
    return x * lax.rsqrt(ms + EPS) * g


def _normmod(x, g, shift, scale):
    return _rms(x, g) * (1.0 + scale) + shift


def _dot(a, b):
    return jnp.dot(a, b, preferred_element_type=F32)


def _dot_nt(a, b):
    return lax.dot_general(a, b, (((1,), (1,)), ((), ())), preferred_element_type=F32)


def _dot_tn(a, b):
    return lax.dot_general(a, b, (((0,), (0,)), ((), ())), preferred_element_type=F32)


def _ada_kernel(c_ref, w_ref, b_ref, o_ref):
    s = _silu(c_ref[...]).astype(BF16)
    o_ref[0] = _dot(s, w_ref[0].astype(BF16)) + b_ref[0]


def _ada(cvec, w_ada, b_ada):
    depth, d, n = w_ada.shape
    r = cvec.shape[0]
    bn = _tile(n, BN_ADA, 128)
    return pl.pallas_call(
        _ada_kernel,
        grid=(depth, n // bn),
        in_specs=[pl.BlockSpec((r, d), lambda l, j: (0, 0)),
                  pl.BlockSpec((1, d, bn), lambda l, j: (l, 0, j)),
                  pl.BlockSpec((1, 1, bn), lambda l, j: (l, 0, j))],
        out_specs=pl.BlockSpec((1, r, bn), lambda l, j: (l, 0, j)),
        out_shape=jax.ShapeDtypeStruct((depth, r, n), F32),
        compiler_params=_params("arbitrary", "arbitrary"),
        name="ada",
    )(cvec, w_ada, b_ada.reshape(depth, 1, n))


def _normmod_kernel(x_ref, g_ref, sh_ref, sc_ref, o_ref):
    o_ref[...] = _normmod(x_ref[...], g_ref[...], sh_ref[0], sc_ref[0]).astype(o_ref.dtype)


def _normmod_call(x, g, shift, scale, rows_per_mod):
    m, d = x.shape
    bm = _tile(rows_per_mod, BM_NORM, 16)
    per = rows_per_mod // bm
    vec = pl.BlockSpec((1, 1, d), lambda i: (i // per, 0, 0))
    return pl.pallas_call(
        _normmod_kernel,
        grid=(m // bm,),
        in_specs=[pl.BlockSpec((bm, d), lambda i: (i, 0)),
                  pl.BlockSpec((1, d), lambda i: (0, 0)), vec, vec],
        out_specs=pl.BlockSpec((bm, d), lambda i: (i, 0)),
        out_shape=jax.ShapeDtypeStruct((m, d), BF16),
        compiler_params=_params("parallel"),
        name="normmod",
    )(x, g.reshape(1, d), shift, scale)


def _proj_kernel(h_ref, w_ref, o_ref):
    o_ref[...] = _dot(h_ref[...], w_ref[...]).astype(o_ref.dtype)


def _proj_scale_kernel(h_ref, w_ref, s_ref, o_ref):
    o_ref[...] = (_dot(h_ref[...], w_ref[...]) * s_ref[...]).astype(o_ref.dtype)


def _swap32(r):
    lane = lax.broadcasted_iota(jnp.int32, r.shape, 1)
    return jnp.where((lane & 32) == 0, pltpu.roll(r, 96, axis=1), pltpu.roll(r, 32, axis=1))


def _row_parts(bm, rows):
    rb = rows if bm % rows == 0 else bm
    return [(r, rb) for r in range(0, bm, rb)]


def _proj_rope_kernel(h_ref, w_ref, c_ref, s_ref, o_ref):
    for r0, rb in _row_parts(h_ref.shape[0], ROWS_PART):
        rs = slice(r0, r0 + rb)
        res = _dot(h_ref[rs, :], w_ref[...])
        cos, sin = c_ref[0, rs, :], s_ref[0, rs, :]
        for g in range(res.shape[1] // 128):
            r = res[:, g * 128:(g + 1) * 128]
            o_ref[rs, g * 128:(g + 1) * 128] = (r * cos + _swap32(r) * sin).astype(o_ref.dtype)


def _col_block_map(ranges, bn):
    assert all(first % bn == 0 and cnt % bn == 0 for first, cnt in ranges), (ranges, bn)

    def block(j):
        blk = j + ranges[0][0] // bn
        seen = 0
        for (first, cnt), (nxt, _) in zip(ranges[:-1], ranges[1:]):
            seen += cnt // bn
            blk = blk + jnp.where(j >= seen, (nxt - first - cnt) // bn, 0)
        return blk
    return block


def _proj(h, w, layer, ranges, out_dtype=None, colscale=None, rope=None, seq=None, name="proj"):
    out_dtype = out_dtype or BF16
    m, d = h.shape
    n = sum(cnt for _, cnt in ranges)
    bm = _tile(m if seq is None else seq, BM_PROJ, 16)
    bn = _tile(n, BN_PROJ, 128) if rope is None else n // 2
    block = _col_block_map(ranges, bn)
    in_specs = [pl.BlockSpec((bm, d), lambda i, j: (i, 0)),
                pl.BlockSpec((None, d, bn), lambda i, j: (layer, 0, block(j)))]
    args = [h, w]
    kern = _proj_kernel
    if colscale is not None:
        kern = _proj_scale_kernel
        in_specs.append(pl.BlockSpec((1, bn), lambda i, j: (0, j)))
        args.append(colscale.reshape(1, n))
    if rope is not None:
        kern = _proj_rope_kernel
        cos, sin = rope
        assert cos.shape == (n // bn, seq, 128)
        per = seq // bm
        tab = pl.BlockSpec((1, bm, 128), lambda i, j: (j, i % per, 0))
        in_specs += [tab, tab]
        args += [cos, sin]
    return pl.pallas_call(
        kern,
        grid=(m // bm, n // bn),
        in_specs=in_specs,
        out_specs=pl.BlockSpec((bm, bn), lambda i, j: (i, j)),
        out_shape=jax.ShapeDtypeStruct((m, n), out_dtype),
        compiler_params=_params("parallel", "arbitrary"),
        name=name,
    )(*args)


def _fill_halo(hp_ref, hc_ref, hn_ref, hext, first, last, bm):
    hext[0:HALO, :] = hp_ref[...]
    hext[HALO:HALO + bm, :] = hc_ref[...]
    hext[HALO + bm:, :] = hn_ref[...]

    @pl.when(first)
    def _():
        hext[0:HALO, :] = jnp.zeros((HALO, hext.shape[1]), hext.dtype)

    @pl.when(last)
    def _():
        hext[HALO + bm:, :] = jnp.zeros((HALO, hext.shape[1]), hext.dtype)


def _conv3(u, cw, cb, bm):
    rows = u.shape[0]
    prev = pltpu.roll(u, 1, axis=0)[HALO:HALO + bm]
    cur = u[HALO:HALO + bm]
    nxt = pltpu.roll(u, rows - 1, axis=0)[HALO:HALO + bm]
    return cb + prev * cw[0:1] + cur * cw[1:2] + nxt * cw[2:3]


def _convproj_silu_kernel(hp_ref, hc_ref, hn_ref, w_ref, cw_ref, cb_ref, s_ref, o_ref, hext,
                          *, bm, per):
    i, j = pl.program_id(0), pl.program_id(1)

    @pl.when(j == 0)
    def _():
        _fill_halo(hp_ref, hc_ref, hn_ref, hext, i % per == 0, i % per == per - 1, bm)

    for r, rb in _row_parts(bm, ROWS_PART):
        u = _dot(hext[r:r + rb + 2 * HALO, :], w_ref[...])
        a = _conv3(u, cw_ref[...], cb_ref[...], rb)
        o_ref[r:r + rb, :] = (_silu(a) * s_ref[...]).astype(o_ref.dtype)


def _convproj_glu_kernel(hp_ref, hc_ref, hn_ref, wa_ref, wg_ref, cwa_ref, cba_ref, cwg_ref, cbg_ref,
                         o_ref, hext, *, bm, per):
    i, j = pl.program_id(0), pl.program_id(1)

    @pl.when(j == 0)
    def _():
        _fill_halo(hp_ref, hc_ref, hn_ref, hext, i % per == 0, i % per == per - 1, bm)

    hx = hext[...]
    a = _conv3(_dot(hx, wa_ref[...]), cwa_ref[...], cba_ref[...], bm)
    g = _conv3(_dot(hx, wg_ref[...]), cwg_ref[...], cbg_ref[...], bm)
    o_ref[...] = (_silu(a) * g).astype(o_ref.dtype)


def _convproj(h, w, layer, col0, n, cw, cb, seq, colscale=None, glu=False, name="convproj"):
    m, d = h.shape
    bm = _tile(seq, BM_CONV, HALO)
    bn = _tile(n, BN_CONV, 128)
    assert col0 % bn == 0
    first = col0 // bn
    per = seq // bm
    nb = bm // HALO
    last_blk = m // HALO - 1
    hspecs = [pl.BlockSpec((HALO, d), lambda i, j: (jnp.maximum(i * nb - 1, 0), 0)),
              pl.BlockSpec((bm, d), lambda i, j: (i, 0)),
              pl.BlockSpec((HALO, d), lambda i, j: (jnp.minimum((i + 1) * nb, last_blk), 0))]
    wspec = lambda off: pl.BlockSpec((None, d, bn), lambda i, j: (layer, 0, first + j + off))
    cwspec = lambda off: pl.BlockSpec((CONV_W, bn), lambda i, j: (0, j + off))
    cbspec = lambda off: pl.BlockSpec((1, bn), lambda i, j: (0, j + off))
    cb2 = cb.reshape(1, -1)
    if glu:
        off = n // bn
        kern = functools.partial(_convproj_glu_kernel, bm=bm, per=per)
        in_specs = hspecs + [wspec(0), wspec(off), cwspec(0), cbspec(0), cwspec(off), cbspec(off)]
        args = [h, h, h, w, w, cw, cb2, cw, cb2]
    else:
        kern = functools.partial(_convproj_silu_kernel, bm=bm, per=per)
        in_specs = hspecs + [wspec(0), cwspec(0), cbspec(0), cbspec(0)]
        args = [h, h, h, w, cw, cb2, colscale.reshape(1, n)]
    return pl.pallas_call(
        kern,
        grid=(m // bm, n // bn),
        in_specs=in_specs,
        out_specs=pl.BlockSpec((bm, bn), lambda i, j: (i, j)),
        out_shape=jax.ShapeDtypeStruct((m, n), BF16),
        scratch_shapes=[pltpu.VMEM((bm + 2 * HALO, d), BF16)],
        compiler_params=_params("parallel", "arbitrary"),
        name=name,
    )(*args)


def _chunk(c):
    return pl.ds(pl.multiple_of(c * CHUNK, CHUNK), CHUNK)


def _retention_kernel(lg_ref, qc_ref, kc_ref, vc_ref, ql_ref, kl_ref, vl_ref, g_ref,
                      oc_ref, ol_ref, st):
    h = pl.program_id(1)
    L = CHUNK
    dk = qc_ref.shape[-1]
    dv = vc_ref.shape[-1]
    assert dk == L
    nc = qc_ref.shape[1] // L
    nl = ql_ref.shape[1] // L
    lgf, lgb = lg_ref[0, h], lg_ref[1, h]
    row = lax.broadcasted_iota(jnp.int32, (L, L), 0).astype(F32)
    col = lax.broadcasted_iota(jnp.int32, (L, L), 1).astype(F32)
    rel = row - col
    mask = (jnp.where(rel >= 0, jnp.exp(jnp.maximum(rel, 0.0) * lgf), 0.0)
            + jnp.where(rel <= 0, jnp.exp(jnp.maximum(-rel, 0.0) * lgb), 0.0))
    qdec_f = jnp.exp((row + 1.0) * lgf)
    kdec_f = jnp.exp((L - 1.0 - row) * lgf)
    qdec_b = jnp.exp((L - row) * lgb)
    kdec_b = jnp.exp(row * lgb)
    cdec_f = jnp.exp(jnp.full((1, dv), L, F32) * lgf)
    cdec_b = jnp.exp(jnp.full((1, dv), L, F32) * lgb)

    def phase_a(k_ref, v_ref, n, base, carry):
        def body(t, carry):
            Sf, Sb = carry
            cf, cb = t, n - 1 - t
            st[base + cf, 0:dk, :] = Sf.astype(BF16)
            st[base + cb, dk:, :] = Sb.astype(BF16)
            kf = (k_ref[0, _chunk(cf), :].astype(F32) * kdec_f).astype(BF16)
            kb = (k_ref[0, _chunk(cb), :].astype(F32) * kdec_b).astype(BF16)
            Sf = cdec_f * Sf + _dot_tn(kf, v_ref[0, _chunk(cf), :])
            Sb = cdec_b * Sb + _dot_tn(kb, v_ref[0, _chunk(cb), :])
            return Sf, Sb
        return lax.fori_loop(0, n, body, carry, unroll=SCAN_UNROLL)

    def phase_b(q_ref, k_ref, v_ref, o_ref, n, base):
        g = g_ref[...]

        def body(c, carry):
            sl = _chunk(c)
            q, k, v = q_ref[0, sl, :], k_ref[0, sl, :], v_ref[0, sl, :]
            s = (_dot_nt(q, k) * mask).astype(BF16)
            qf = q.astype(F32)
            lhs = jnp.concatenate([s, (qf * qdec_f).astype(BF16), (qf * qdec_b).astype(BF16)], axis=1)
            y = _dot(lhs, jnp.concatenate([v, st[base + c]], axis=0))
            o_ref[0, sl, :] = _rms(y, g).astype(o_ref.dtype)
            return carry
        lax.fori_loop(0, n, body, 0, unroll=2 * SCAN_UNROLL)

    zero = jnp.zeros((dk, dv), F32)
    carry = phase_a(kc_ref, vc_ref, nc, 0, (zero, zero))
    phase_a(kl_ref, vl_ref, nl, nc, carry)
    phase_b(qc_ref, kc_ref, vc_ref, oc_ref, nc, 0)
    phase_b(ql_ref, kl_ref, vl_ref, ol_ref, nl, nc)


def _retention(log_gamma, qk_c, pv_c, qk_l, pv_l, g, v_blk, batch):
    tc, tl = qk_c.shape[1], qk_l.shape[1]
    dk = qk_c.shape[2] // (2 * H_R)
    dv = g.shape[0] // H_R
    qs = lambda t: pl.BlockSpec((1, t, dk), lambda b, h: (b, 0, h))
    ks = lambda t: pl.BlockSpec((1, t, dk), lambda b, h: (b, 0, H_R + h))
    vs = lambda t: pl.BlockSpec((1, t, dv), lambda b, h: (b, 0, v_blk + h))
    os_ = lambda t: pl.BlockSpec((1, t, dv), lambda b, h: (b, 0, h))
    return pl.pallas_call(
        _retention_kernel,
        grid=(batch, H_R),
        in_specs=[pl.BlockSpec(memory_space=pltpu.SMEM),
                  qs(tc), ks(tc), vs(tc), qs(tl), ks(tl), vs(tl),
                  pl.BlockSpec((1, dv), lambda b, h: (0, h))],
        out_specs=[os_(tc), os_(tl)],
        out_shape=[jax.ShapeDtypeStruct((batch, tc, H_R * dv), BF16),
                   jax.ShapeDtypeStruct((batch, tl, H_R * dv), BF16)],
        scratch_shapes=[pltpu.VMEM(((tc + tl) // CHUNK, 2 * dk, dv), BF16)],
        compiler_params=_params("parallel", "arbitrary"),
        name="retention",
    )(log_gamma, qk_c, qk_c, pv_c, qk_l, qk_l, pv_l, g.reshape(1, -1))


def _lane_scan(x, reverse, op, identity):
    n = x.shape[-1]
    lane = lax.broadcasted_iota(jnp.int32, x.shape, 1)
    s = 1
    while s < n:
        if reverse:
            x = op(x, jnp.where(lane < n - s, pltpu.roll(x, n - s, axis=1), identity))
        else:
            x = op(x, jnp.where(lane >= s, pltpu.roll(x, s, axis=1), identity))
        s *= 2
    return x


_B, _R, _CM, _W, _BE, _ML = range(6)


def _col_of_row(r):
    return jnp.broadcast_to(r, (r.shape[1], r.shape[1])).T


def _mlstm_kernel(gb_ref, qc_ref, kc_ref, vc_ref, gic_ref, gfc_ref, gibc_ref, gfbc_ref,
                  ql_ref, kl_ref, vl_ref, gil_ref, gfl_ref, gibl_ref, gfbl_ref, g_ref,
                  oc_ref, ol_ref, cst, mst, kt, sfc, sbc, sfl, sbl):
    h = pl.program_id(1)
    L = CHUNK
    dk = qc_ref.shape[-1]
    dv = vc_ref.shape[-1]
    assert dk == L
    nc = qc_ref.shape[1] // L
    nl = ql_ref.shape[1] // L
    row = lax.broadcasted_iota(jnp.int32, (L, L), 0)
    col = lax.broadcasted_iota(jnp.int32, (L, L), 1)
    tri = (col <= row, col >= row)
    end_lane = (L - 1, 0)

    ones_col = jnp.ones((L, L), BF16)

    def v_aug(v_ref, c):
        return jnp.concatenate([v_ref[0, _chunk(c), :], ones_col], axis=1)

    def prep(gi_ref, gf_ref, scr, d):
        rev = d == 1
        ig = (gi_ref[0, 0] + gb_ref[2 * d, h]) * LOG2E
        lf = _log_sigmoid(gf_ref[0, 0] + gb_ref[2 * d + 1, h]) * LOG2E
        b = _lane_scan(lf, rev, jnp.add, 0.0)
        r = ig - b
        lane = lax.broadcasted_iota(jnp.int32, b.shape, 1)
        be = jnp.broadcast_to(
            jnp.sum(jnp.where(lane == end_lane[d], b, 0.0), axis=-1, keepdims=True), b.shape)
        loc = be + r
        m_loc = jnp.broadcast_to(jnp.max(loc, axis=-1, keepdims=True), b.shape)
        scr[_B] = b
        scr[_R] = r
        scr[_CM] = _lane_scan(r, rev, jnp.maximum, -jnp.inf)
        scr[_W] = jnp.exp2(loc - m_loc)
        scr[_BE] = be
        scr[_ML] = m_loc

    prep(gic_ref, gfc_ref, sfc, 0)
    prep(gibc_ref, gfbc_ref, sbc, 1)
    prep(gil_ref, gfl_ref, sfl, 0)
    prep(gibl_ref, gfbl_ref, sbl, 1)

    def transpose_k(k_ref, n, base):
        def body(c, carry):
            kt[base + c] = k_ref[0, _chunk(c), :].T
            return carry
        lax.fori_loop(0, n, body, 0, unroll=SCAN_UNROLL)

    transpose_k(kc_ref, nc, 0)
    transpose_k(kl_ref, nl, nc)

    def phase_a(v_ref, scr, n, base, carry):
        def step(d, c, state):
            C, m = state
            s = scr[d]
            cst[d, base + c] = C.astype(BF16)
            mst[d, base + c] = m
            row_of = lambda idx: s[idx, pl.ds(c, 1), :]
            kw = (kt[base + c].astype(F32) * row_of(_W)).astype(BF16)
            dC = _dot(kw, v_aug(v_ref, c))
            be, m_loc = row_of(_BE), row_of(_ML)
            m_new = jnp.maximum(be + m, m_loc)
            f_old = jnp.exp2(be + m - m_new)
            f_new = jnp.exp2(m_loc - m_new)
            wide = lambda f: jnp.concatenate([f] * (C.shape[1] // L), axis=1)
            return wide(f_old) * C + wide(f_new) * dC, m_new

        def body(t, carry):
            return step(0, t, carry[0]), step(1, n - 1 - t, carry[1])
        return lax.fori_loop(0, n, body, carry, unroll=SCAN_UNROLL)

    def phase_b(q_ref, k_ref, v_ref, o_ref, scr, n, base):
        g = g_ref[...]
        ones_dv = jnp.ones((dv, L), BF16)

        def body(c, carry):
            sl = _chunk(c)
            q, k = q_ref[0, sl, :], k_ref[0, sl, :]
            va = v_aug(v_ref, c)
            qk = _dot_nt(q, k)
            qf = q.astype(F32)
            y = None
            for d in (0, 1):
                s = scr[d]
                row_of = lambda idx: s[idx, pl.ds(c, 1), :]
                m_prev = mst[d, base + c]
                m_t = row_of(_B) + jnp.maximum(row_of(_CM), m_prev)
                U = _col_of_row(row_of(_B) - m_t)
                NM = _col_of_row(-m_t)
                p = jnp.exp2(jnp.where(tri[d], U + row_of(_R), -jnp.inf))
                a = jnp.exp2(U + m_prev)
                lhs = jnp.concatenate([(qk * p).astype(BF16), (a * qf).astype(BF16)], axis=1)
                rhs = jnp.concatenate([va, cst[d, base + c]], axis=0)
                res = _dot(lhs, rhs)
                inv = 1.0 / jnp.maximum(jnp.abs(res[:, dv:]), jnp.exp2(NM))
                hval = res[:, :dv] * jnp.concatenate([inv] * (dv // L), axis=1)
                y = hval if y is None else y + hval
            ms = _dot((y * y).astype(BF16), ones_dv) * (1.0 / dv)
            scale = lax.rsqrt(ms + EPS)
            o_ref[0, sl, :] = (y * jnp.concatenate([scale] * (dv // L), axis=1) * g).astype(o_ref.dtype)
            return carry
        lax.fori_loop(0, n, body, 0, unroll=2 * SCAN_UNROLL)

    zero = (jnp.zeros((dk, dv + L), F32), jnp.zeros((1, L), F32))
    scr_c = (sfc, sbc)
    scr_l = (sfl, sbl)
    carry = phase_a(vc_ref, scr_c, nc, 0, (zero, zero))
    phase_a(vl_ref, scr_l, nl, nc, carry)
    phase_b(qc_ref, kc_ref, vc_ref, oc_ref, scr_c, nc, 0)
    phase_b(ql_ref, kl_ref, vl_ref, ol_ref, scr_l, nl, nc)


def _mlstm(gate_b, qk_c, pv_c, gates_c, qk_l, pv_l, gates_l, g, v_blk, batch):
    tc, tl = qk_c.shape[1], qk_l.shape[1]
    dk = qk_c.shape[2] // (2 * H_M)
    dv = g.shape[0] // H_M
    qs = lambda t: pl.BlockSpec((1, t, dk), lambda b, h: (b, 0, h))
    ks = lambda t: pl.BlockSpec((1, t, dk), lambda b, h: (b, 0, H_M + h))
    vs = lambda t: pl.BlockSpec((1, t, dv), lambda b, h: (b, 0, v_blk + h))
    gs = lambda t, k: pl.BlockSpec((1, 1, t // CHUNK, CHUNK), lambda b, h: (b, k * H_M + h, 0, 0))
    os_ = lambda t: pl.BlockSpec((1, t, dv), lambda b, h: (b, 0, h))

    def stream(t):
        return [qs(t), ks(t), vs(t), gs(t, 0), gs(t, 1), gs(t, 2), gs(t, 3)]

    n_slots = (tc + tl) // CHUNK
    rows = lambda t: pltpu.VMEM((6, t // CHUNK, CHUNK), F32)
    return pl.pallas_call(
        _mlstm_kernel,
        grid=(batch, H_M),
        in_specs=[pl.BlockSpec(memory_space=pltpu.SMEM)] + stream(tc) + stream(tl)
                 + [pl.BlockSpec((1, dv), lambda b, h: (0, h))],
        out_specs=[os_(tc), os_(tl)],
        out_shape=[jax.ShapeDtypeStruct((batch, tc, H_M * dv), BF16),
                   jax.ShapeDtypeStruct((batch, tl, H_M * dv), BF16)],
        scratch_shapes=[pltpu.VMEM((2, n_slots, dk, dv + CHUNK), BF16),
                        pltpu.VMEM((2, n_slots, 1, CHUNK), F32),
                        pltpu.VMEM((n_slots, dk, CHUNK), BF16),
                        rows(tc), rows(tc), rows(tl), rows(tl)],
        compiler_params=_params("parallel", "arbitrary"),
        name="mlstm",
    )(gate_b, qk_c, qk_c, pv_c, gates_c, gates_c, gates_c, gates_c,
      qk_l, qk_l, pv_l, gates_l, gates_l, gates_l, gates_l, g.reshape(1, -1))


def _merge_kernel(yr_ref, rg_ref, ym_ref, mo_ref, gr_ref, gm_ref, wr_ref, wm_ref, o_ref):
    a = (yr_ref[...] * _silu(rg_ref[...].astype(F32)).astype(BF16))
    b = (ym_ref[...] * _sigmoid(mo_ref[...].astype(F32)).astype(BF16))
    y = (_sigmoid(gr_ref[...].astype(F32)) * _dot(a, wr_ref[...])
         + _sigmoid(gm_ref[...].astype(F32)) * _dot(b, wm_ref[...]))
    o_ref[...] = y.astype(o_ref.dtype)


def _merge(yr, ym, pa, pb, wr, wm, layer):
    m, vw = yr.shape
    d = wr.shape[2]
    bm = _tile(m, BM_ROW, 16)
    col = lambda k: pl.BlockSpec((bm, vw), lambda i: (i, k))
    wspec = pl.BlockSpec((None, vw, d), lambda i: (layer, 0, 0), pipeline_mode=pl.Buffered(1))
    return pl.pallas_call(
        _merge_kernel,
        grid=(m // bm,),
        in_specs=[col(0), col(1), col(0), col(3), col(0), col(1), wspec, wspec],
        out_specs=pl.BlockSpec((bm, d), lambda i: (i, 0)),
        out_shape=jax.ShapeDtypeStruct((m, d), BF16),
        compiler_params=_params("parallel"),
        name="merge",
    )(yr, pa, ym, pa, pb, pb, wr, wm)


def _resid_kernel(a_ref, w_ref, x_ref, go_ref, gate_ref, gn_ref, sh_ref, sc_ref, xo_ref, ho_ref):
    for r, rb in _row_parts(a_ref.shape[0], BM_ROW):
        rs = slice(r, r + rb)
        out = _dot(a_ref[rs, :], w_ref[...])
        xn = x_ref[rs, :] + gate_ref[0] * _rms(out, go_ref[...])
        xo_ref[rs, :] = xn
        ho_ref[rs, :] = _normmod(xn, gn_ref[...], sh_ref[0], sc_ref[0]).astype(ho_ref.dtype)


def _resid_last_kernel(a_ref, w_ref, x_ref, go_ref, gate_ref, xo_ref):
    for r, rb in _row_parts(a_ref.shape[0], BM_ROW):
        rs = slice(r, r + rb)
        out = _dot(a_ref[rs, :], w_ref[...])
        xo_ref[rs, :] = x_ref[rs, :] + gate_ref[0] * _rms(out, go_ref[...])


def _resid(a, w, layer, x, g_out, gate, rows_per_mod, nxt=None, name="resid"):
    m, k = a.shape
    d = w.shape[2]
    bm = _tile(rows_per_mod, 2 * BM_ROW, 16)
    tiles = 2 * bm * (2 * k + 4 * d + 4 * d + 2 * d) + bm * 4 * d
    if 2 * k * d + tiles > V7X_VMEM_LIMIT:
        bm = _tile(rows_per_mod, BM_ROW, 16)
    per = rows_per_mod // bm
    row = lambda width: pl.BlockSpec((bm, width), lambda i: (i, 0))
    vec = pl.BlockSpec((1, 1, d), lambda i: (i // per, 0, 0))
    gvec = pl.BlockSpec((1, d), lambda i: (0, 0))
    in_specs = [row(k),
                pl.BlockSpec((None, k, d), lambda i: (layer, 0, 0), pipeline_mode=pl.Buffered(1)),
                row(d), gvec, vec]
    args = [a, w, x, g_out.reshape(1, d), gate]
    if nxt is None:
        return pl.pallas_call(
            _resid_last_kernel, grid=(m // bm,), in_specs=in_specs, out_specs=row(d),
            out_shape=jax.ShapeDtypeStruct((m, d), F32),
            compiler_params=_params("parallel"), name=name,
        )(*args)
    g_next, shift, scale = nxt
    return pl.pallas_call(
        _resid_kernel, grid=(m // bm,),
        in_specs=in_specs + [gvec, vec, vec],
        out_specs=[row(d), row(d)],
        out_shape=[jax.ShapeDtypeStruct((m, d), F32), jax.ShapeDtypeStruct((m, d), BF16)],
        compiler_params=_params("parallel"), name=name,
    )(*args, g_next.reshape(1, d), shift, scale)


def _rope_tables(seq, dk, q_scale):
    quarter = dk // 4
    t = jnp.arange(seq)
    rows = (t // GRID_W).astype(F32)
    cols = (t % GRID_W).astype(F32)
    freqs = ROPE_BASE ** (-jnp.arange(quarter, dtype=F32) / quarter)
    ar = rows[:, None] * freqs[None, :]
    ac = cols[:, None] * freqs[None, :]
    cos = jnp.concatenate([jnp.cos(ar), jnp.cos(ar), jnp.cos(ac), jnp.cos(ac)], axis=-1)
    sin = jnp.concatenate([-jnp.sin(ar), jnp.sin(ar), -jnp.sin(ac), jnp.sin(ac)], axis=-1)
    scale = jnp.array([q_scale, 1.0], F32)[:, None, None]
    return cos[None] * scale, sin[None] * scale


def _gates_t_kernel(wt_ref, h_ref, o_ref):
    o_ref[0] = _dot_nt(wt_ref[...], h_ref[...])


def _gates_t(h, wt, layer, batch, seq, name):
    m, d = h.shape
    ng = wt.shape[1]
    bm = _tile(seq, BM_PROJ, 128)
    per = seq // bm
    out = pl.pallas_call(
        _gates_t_kernel,
        grid=(m // bm,),
        in_specs=[pl.BlockSpec((None, ng, d), lambda i: (layer, 0, 0)),
                  pl.BlockSpec((bm, d), lambda i: (i, 0))],
        out_specs=pl.BlockSpec((1, ng, bm), lambda i: (i // per, 0, i % per)),
        out_shape=jax.ShapeDtypeStruct((batch, ng, seq), F32),
        compiler_params=_params("parallel"),
        name=name,
    )(wt, h)
    return out.reshape(batch, ng, seq // CHUNK, CHUNK)


def kernel(x, c, ctx, c_ctx, w_ada, b_ada, norm_w, w_in, mlstm_conv_w, mlstm_conv_b, mlstm_gate_b,
           ret_decay_exp, head_norm_w, w_ret_out, w_mlstm_out, w_o, w_up, ffn_conv_w, ffn_conv_b,
           w_down):
    batch, seq, d = x.shape
    ctx_len = ctx.shape[1]
    depth = w_ada.shape[0]
    vw = w_ret_out.shape[1]
    r_qk = (w_in.shape[2] - 4 * vw - 4 * H_M - 2 * d) // 4
    r_dk = r_qk // H_R
    m_dk = r_qk // H_M
    assert d == vw, "column-block addressing of the projection output assumes D_MODEL == VW"

    o_rv = 2 * r_qk
    o_mq = o_rv + 2 * vw
    o_mv = o_mq + 2 * r_qk
    o_mg = o_mv + 2 * vw
    o_merge = o_mg + 4 * H_M

    pad = (-(batch + 1)) % 8
    cvec = jnp.concatenate([c, c_ctx[None], jnp.zeros((pad, d), F32)], axis=0)
    mod = _ada(cvec, w_ada, b_ada)

    def mod_l(l, k):
        return mod[l, :batch, k * d:(k + 1) * d].reshape(batch, 1, d)

    def mod_c(l, k):
        return mod[l, batch:batch + 1, k * d:(k + 1) * d].reshape(1, 1, d)

    rope = _rope_tables(seq, r_dk, r_dk ** -0.5)
    qscale_r = jnp.concatenate([jnp.full((r_qk,), r_dk ** -0.5, F32), jnp.ones((r_qk,), F32)])
    qscale_m = jnp.concatenate([jnp.full((r_qk,), m_dk ** -0.5, F32), jnp.ones((r_qk,), F32)])
    log_gamma = jnp.log1p(-jnp.exp2(-ret_decay_exp.astype(F32)))

    ml, mc = batch * seq, batch * ctx_len
    xl = x.reshape(ml, d)
    xc = ctx.reshape(mc, d)
    hl = _normmod_call(xl, norm_w[0, 0], mod_l(0, 0), mod_l(0, 1), seq)
    hc = _normmod_call(xc, norm_w[0, 0], mod_c(0, 0), mod_c(0, 1), mc)

    w_in_b = w_in.astype(BF16)
    w_mg_b = w_in_b[:, :, o_merge:]
    w_gate_t = jnp.transpose(w_in_b[:, :, o_mg:o_merge], (0, 2, 1))
    wr_b, wm_b, wo_b, wu_b, wd_b = (w.astype(BF16) for w in (w_ret_out, w_mlstm_out, w_o, w_up, w_down))
    d_ff = w_down.shape[1]

    for l in range(depth):
        last = l == depth - 1
        cw, cb = mlstm_conv_w[l], mlstm_conv_b[l]

        def project(h, seq_len, rope_tabs, nm):
            pa = _proj(h, w_in_b, l, ((o_rv, 2 * vw), (o_mv, 2 * vw)), name=nm + "_plain")
            pb = _proj(h, w_mg_b, l, ((0, 2 * d),), name=nm + "_mgate")
            if rope_tabs is None:
                rqk = _proj(h, w_in_b, l, ((0, o_rv),), colscale=qscale_r, name=nm + "_rqk")
            else:
                rqk = _proj(h, w_in_b, l, ((0, o_rv),), rope=rope_tabs, seq=seq_len, name=nm + "_rqk")
            mqk = _convproj(h, w_in_b, l, o_mq, 2 * r_qk, cw, cb, seq_len, colscale=qscale_m,
                            name=nm + "_mqk")
            b = h.shape[0] // seq_len
            return (pa, pb, rqk.reshape(b, seq_len, -1), mqk.reshape(b, seq_len, -1),
                    _gates_t(h, w_gate_t, l, b, seq_len, nm + "_gates"))

        pa_c, pb_c, rqk_c, mqk_c, g_c = project(hc, ctx_len, None, "ctx")
        pa_l, pb_l, rqk_l, mqk_l, g_l = project(hl, seq, rope, "lat")
        pv_c = pa_c.reshape(batch, ctx_len, -1)
        pv_l = pa_l.reshape(batch, seq, -1)

        yr_c, yr_l = _retention(log_gamma[l], rqk_c, pv_c, rqk_l, pv_l, head_norm_w[l, 0],
                                0 * H_R, batch)
        ym_c, ym_l = _mlstm(mlstm_gate_b[l], mqk_c, pv_c, g_c, mqk_l, pv_l, g_l, head_norm_w[l, 1],
                            2 * H_M, batch)

        def tail(yr, ym, pa, pb, xres, mod_k, rows_per_mod, seq_len, nm):
            y = _merge(yr.reshape(-1, vw), ym.reshape(-1, vw), pa, pb, wr_b, wm_b, l)
            x1, h2 = _resid(y, wo_b, l, xres, norm_w[l, 1], mod_k(l, 2), rows_per_mod,
                            nxt=(norm_w[l, 2], mod_k(l, 3), mod_k(l, 4)), name=nm + "_o")
            act = _convproj(h2, wu_b, l, 0, d_ff, ffn_conv_w[l], ffn_conv_b[l], seq_len, glu=True,
                            name=nm + "_ffn")
            if last:
                return _resid(act, wd_b, l, x1, norm_w[l, 3], mod_k(l, 5), rows_per_mod,
                              name=nm + "_down"), None
            return _resid(act, wd_b, l, x1, norm_w[l, 3], mod_k(l, 5), rows_per_mod,
                          nxt=(norm_w[l + 1, 0], mod_k(l + 1, 0), mod_k(l + 1, 1)), name=nm + "_down")

        xl, hl = tail(yr_l, ym_l, pa_l, pb_l, xl, mod_l, seq, seq, "lat")
        if not last:
            xc, hc = tail(yr_c, ym_c, pa_c, pb_c, xc, mod_c, mc, ctx_len, "ctx")

    return xl.reshape(batch, seq, d)
```

```python
import functools

import jax
import jax.numpy as jnp
from jax import lax
from jax.experimental import pallas as pl
from jax.experimental.pallas import tpu as pltpu

F32 = jnp.float32
BF16 = jnp.bfloat16

GRID_W = 64
CHUNK = 128
H_R = 8
H_M = 8
N_MOD = 6
CONV_W = 3
ROPE_BASE = 10000.0
EPS = 1e-6
LOG2E = 1.4426950408889634
HALO = 16
ROWS_PART = 512
V7X_VMEM_LIMIT = 56 * 1024 * 1024

BM_PROJ = 1024
BN_PROJ = 1024
BM_CONV = 1024
BN_CONV = 512
BM_ROW = 256
BM_NORM = 512
BN_ADA = 1024
SCAN_UNROLL = 4


def _tile(n, pref, quantum):
    if n <= pref:
        return n
    t = (pref // quantum) * quantum
    while t > quantum and n % t:
        t -= quantum
    assert n % t == 0, (n, pref, quantum)
    return t


def _params(*sem):
    return pltpu.CompilerParams(dimension_semantics=sem, vmem_limit_bytes=V7X_VMEM_LIMIT)


def _sigmoid(x):
    return 1.0 / (1.0 + jnp.exp(-x))


def _silu(x):
    return x * _sigmoid(x)


def _log_sigmoid(x):
    return jnp.minimum(x, 0.0) - jnp.log1p(jnp.exp(-jnp.abs(x)))


def _rms(x, g):
    ms = jnp.mean(x * x, axis=-1, keepdims=True)
    return x * lax.rsqrt(ms + EPS) * g


def _normmod(x, g, shift, scale):
    return _rms(x, g) * (1.0 + scale) + shift


def _dot(a, b):
    return jnp.dot(a, b, preferred_element_type=F32)


def _dot_nt(a, b):
    return lax.dot_general(a, b, (((1,), (1,)), ((), ())), preferred_element_type=F32)


def _dot_tn(a, b):
    return lax.dot_general(a, b, (((0,), (0,)), ((), ())), preferred_element_type=F32)


def _ada_kernel(c_ref, w_ref, b_ref, o_ref):
    s = _silu(c_ref[...]).astype(BF16)
    o_ref[0] = _dot(s, w_ref[0].astype(BF16)) + b_ref[0]


def _ada(cvec, w_ada, b_ada):
    depth, d, n = w_ada.shape
    r = cvec.shape[0]
    bn = _tile(n, BN_ADA, 128)
    return pl.pallas_call(
        _ada_kernel,
        grid=(depth, n // bn),
        in_specs=[pl.BlockSpec((r, d), lambda l, j: (0, 0)),
                  pl.BlockSpec((1, d, bn), lambda l, j: (l, 0, j)),
                  pl.BlockSpec((1, 1, bn), lambda l, j: (l, 0, j))],
        out_specs=pl.BlockSpec((1, r, bn), lambda l, j: (l, 0, j)),
        out_shape=jax.ShapeDtypeStruct((depth, r, n), F32),
        compiler_params=_params("arbitrary", "arbitrary"),
        name="ada",
    )(cvec, w_ada, b_ada.reshape(depth, 1, n))


def _normmod_kernel(x_ref, g_ref, sh_ref, sc_ref, o_ref):
    o_ref[...] = _normmod(x_ref[...], g_ref[...], sh_ref[0], sc_ref[0]).astype(o_ref.dtype)


def _normmod_call(x, g, shift, scale, rows_per_mod):
    m, d = x.shape
    bm = _tile(rows_per_mod, BM_NORM, 16)
    per = rows_per_mod // bm
    vec = pl.BlockSpec((1, 1, d), lambda i: (i // per, 0, 0))
    return pl.pallas_call(
        _normmod_kernel,
        grid=(m // bm,),
        in_specs=[pl.BlockSpec((bm, d), lambda i: (i, 0)),
                  pl.BlockSpec((1, d), lambda i: (0, 0)), vec, vec],
        out_specs=pl.BlockSpec((bm, d), lambda i: (i, 0)),
        out_shape=jax.ShapeDtypeStruct((m, d), BF16),
        compiler_params=_params("parallel"),
        name="normmod",
    )(x, g.reshape(1, d), shift, scale)


def _proj_kernel(h_ref, w_ref, o_ref):
    o_ref[...] = _dot_nt(h_ref[...], w_ref[...]).astype(o_ref.dtype)


def _proj_scale_kernel(h_ref, w_ref, s_ref, o_ref):
    o_ref[...] = (_dot_nt(h_ref[...], w_ref[...]) * s_ref[...]).astype(o_ref.dtype)


def _swap32(r):
    lane = lax.broadcasted_iota(jnp.int32, r.shape, 1)
    return jnp.where((lane & 32) == 0, pltpu.roll(r, 96, axis=1), pltpu.roll(r, 32, axis=1))


def _row_parts(bm, rows):
    rb = rows if bm % rows == 0 else bm
    return [(r, rb) for r in range(0, bm, rb)]


def _proj_rope_kernel(h_ref, w_ref, c_ref, s_ref, o_ref):
    for r0, rb in _row_parts(h_ref.shape[0], ROWS_PART):
        rs = slice(r0, r0 + rb)
        res = _dot_nt(h_ref[rs, :], w_ref[...])
        cos, sin = c_ref[0, rs, :], s_ref[0, rs, :]
        for g in range(res.shape[1] // 128):
            r = res[:, g * 128:(g + 1) * 128]
            o_ref[rs, g * 128:(g + 1) * 128] = (r * cos + _swap32(r) * sin).astype(o_ref.dtype)


def _col_block_map(ranges, bn):
    assert all(first % bn == 0 and cnt % bn == 0 for first, cnt in ranges), (ranges, bn)

    def block(j):
        blk = j + ranges[0][0] // bn
        seen = 0
        for (first, cnt), (nxt, _) in zip(ranges[:-1], ranges[1:]):
            seen += cnt // bn
            blk = blk + jnp.where(j >= seen, (nxt - first - cnt) // bn, 0)
        return blk
    return block


def _proj(h, w, layer, ranges, colscale=None, rope=None, seq=None, name="proj"):
    m, d = h.shape
    n = sum(cnt for _, cnt in ranges)
    bm = _tile(m if seq is None else seq, BM_PROJ, 16)
    bn = _tile(n, BN_PROJ, 128) if rope is None else n // 2
    block = _col_block_map(ranges, bn)
    in_specs = [pl.BlockSpec((bm, d), lambda i, j: (i, 0)),
                pl.BlockSpec((None, bn, d), lambda i, j: (layer, block(j), 0))]
    args = [h, w]
    kern = _proj_kernel
    if colscale is not None:
        kern = _proj_scale_kernel
        in_specs.append(pl.BlockSpec((1, bn), lambda i, j: (0, j)))
        args.append(colscale.reshape(1, n))
    if rope is not None:
        kern = _proj_rope_kernel
        cos, sin = rope
        assert cos.shape == (n // bn, seq, 128)
        per = seq // bm
        tab = pl.BlockSpec((1, bm, 128), lambda i, j: (j, i % per, 0))
        in_specs += [tab, tab]
        args += [cos, sin]
    return pl.pallas_call(
        kern,
        grid=(m // bm, n // bn),
        in_specs=in_specs,
        out_specs=pl.BlockSpec((bm, bn), lambda i, j: (i, j)),
        out_shape=jax.ShapeDtypeStruct((m, n), BF16),
        compiler_params=_params("parallel", "arbitrary"),
        name=name,
    )(*args)


def _fill_halo(hp_ref, hc_ref, hn_ref, hext, first, last, bm):
    hext[0:HALO, :] = hp_ref[...]
    hext[HALO:HALO + bm, :] = hc_ref[...]
    hext[HALO + bm:, :] = hn_ref[...]

    @pl.when(first)
    def _():
        hext[0:HALO, :] = jnp.zeros((HALO, hext.shape[1]), hext.dtype)

    @pl.when(last)
    def _():
        hext[HALO + bm:, :] = jnp.zeros((HALO, hext.shape[1]), hext.dtype)


def _conv3(u, cw, cb, bm):
    rows = u.shape[0]
    prev = pltpu.roll(u, 1, axis=0)[HALO:HALO + bm]
    cur = u[HALO:HALO + bm]
    nxt = pltpu.roll(u, rows - 1, axis=0)[HALO:HALO + bm]
    return cb + prev * cw[0:1] + cur * cw[1:2] + nxt * cw[2:3]


def _convproj_pair_kernel(hp_ref, hc_ref, hn_ref, wa_ref, wg_ref, cwa_ref, cba_ref, cwg_ref, cbg_ref,
                          oa_ref, og_ref, hext, *, bm, per, scale_a, mm):
    i, j = pl.program_id(0), pl.program_id(1)

    @pl.when(j == 0)
    def _():
        _fill_halo(hp_ref, hc_ref, hn_ref, hext, i % per == 0, i % per == per - 1, bm)

    hx = hext[...]
    a = _conv3(mm(hx, wa_ref[...]), cwa_ref[...], cba_ref[...], bm)
    oa_ref[...] = (_silu(a) * scale_a).astype(oa_ref.dtype)
    g = _conv3(mm(hx, wg_ref[...]), cwg_ref[...], cbg_ref[...], bm)
    og_ref[...] = _silu(g).astype(og_ref.dtype)


def _convproj_glu_kernel(hp_ref, hc_ref, hn_ref, wa_ref, wg_ref, cwa_ref, cba_ref, cwg_ref, cbg_ref,
                         o_ref, hext, *, bm, per, mm):
    i, j = pl.program_id(0), pl.program_id(1)

    @pl.when(j == 0)
    def _():
        _fill_halo(hp_ref, hc_ref, hn_ref, hext, i % per == 0, i % per == per - 1, bm)

    hx = hext[...]
    a = _conv3(mm(hx, wa_ref[...]), cwa_ref[...], cba_ref[...], bm)
    g = _conv3(mm(hx, wg_ref[...]), cwg_ref[...], cbg_ref[...], bm)
    o_ref[...] = (_silu(a) * g).astype(o_ref.dtype)


def _convproj(h, w, layer, col0, n, cw, cb, seq, scale_a=None, transposed=False, name="convproj"):
    glu = scale_a is None
    mm = _dot_nt if transposed else _dot
    m, d = h.shape
    bm = _tile(seq, BM_CONV, HALO)
    bn = _tile(n, BN_CONV, 128)
    assert col0 % bn == 0
    first = col0 // bn
    per = seq // bm
    nb = bm // HALO
    last_blk = m // HALO - 1
    hspecs = [pl.BlockSpec((HALO, d), lambda i, j: (jnp.maximum(i * nb - 1, 0), 0)),
              pl.BlockSpec((bm, d), lambda i, j: (i, 0)),
              pl.BlockSpec((HALO, d), lambda i, j: (jnp.minimum((i + 1) * nb, last_blk), 0))]
    if transposed:
        wspec = lambda off: pl.BlockSpec((None, bn, d), lambda i, j: (layer, first + j + off, 0))
    else:
        wspec = lambda off: pl.BlockSpec((None, d, bn), lambda i, j: (layer, 0, first + j + off))
    cwspec = lambda off: pl.BlockSpec((CONV_W, bn), lambda i, j: (0, j + off))
    cbspec = lambda off: pl.BlockSpec((1, bn), lambda i, j: (0, j + off))
    cb2 = cb.reshape(1, -1)
    off = n // bn
    if glu:
        kern = functools.partial(_convproj_glu_kernel, bm=bm, per=per, mm=mm)
    else:
        kern = functools.partial(_convproj_pair_kernel, bm=bm, per=per, scale_a=scale_a, mm=mm)
    out_spec = pl.BlockSpec((bm, bn), lambda i, j: (i, j))
    out_shape = jax.ShapeDtypeStruct((m, n), BF16)
    return pl.pallas_call(
        kern,
        grid=(m // bm, n // bn),
        in_specs=hspecs + [wspec(0), wspec(off), cwspec(0), cbspec(0), cwspec(off), cbspec(off)],
        out_specs=out_spec if glu else [out_spec, out_spec],
        out_shape=out_shape if glu else [out_shape, out_shape],
        scratch_shapes=[pltpu.VMEM((bm + 2 * HALO, d), BF16)],
        compiler_params=_params("parallel", "arbitrary"),
        name=name,
    )(h, h, h, w, w, cw, cb2, cw, cb2)


def _chunk(c):
    return pl.ds(pl.multiple_of(c * CHUNK, CHUNK), CHUNK)


def _retention_kernel(lg_ref, qc_ref, kc_ref, vc_ref, ql_ref, kl_ref, vl_ref, g_ref,
                      oc_ref, ol_ref, st):
    h = pl.program_id(1)
    L = CHUNK
    dk = qc_ref.shape[-1]
    dv = vc_ref.shape[-1]
    assert dk == L
    nc = qc_ref.shape[1] // L
    nl = ql_ref.shape[1] // L
    lgf, lgb = lg_ref[0, h], lg_ref[1, h]
    row = lax.broadcasted_iota(jnp.int32, (L, L), 0).astype(F32)
    col = lax.broadcasted_iota(jnp.int32, (L, L), 1).astype(F32)
    rel = row - col
    mask = (jnp.where(rel >= 0, jnp.exp(jnp.maximum(rel, 0.0) * lgf), 0.0)
            + jnp.where(rel <= 0, jnp.exp(jnp.maximum(-rel, 0.0) * lgb), 0.0))
    qdec_f = jnp.exp((row + 1.0) * lgf)
    kdec_f = jnp.exp((L - 1.0 - row) * lgf)
    qdec_b = jnp.exp((L - row) * lgb)
    kdec_b = jnp.exp(row * lgb)
    cdec_f = jnp.exp(jnp.full((1, dv), L, F32) * lgf)
    cdec_b = jnp.exp(jnp.full((1, dv), L, F32) * lgb)

    def phase_a(k_ref, v_ref, n, base, carry):
        def body(t, carry):
            Sf, Sb = carry
            cf, cb = t, n - 1 - t
            st[base + cf, 0:dk, :] = Sf.astype(BF16)
            st[base + cb, dk:, :] = Sb.astype(BF16)
            kf = (k_ref[0, _chunk(cf), :].astype(F32) * kdec_f).astype(BF16)
            kb = (k_ref[0, _chunk(cb), :].astype(F32) * kdec_b).astype(BF16)
            Sf = cdec_f * Sf + _dot_tn(kf, v_ref[0, _chunk(cf), :])
            Sb = cdec_b * Sb + _dot_tn(kb, v_ref[0, _chunk(cb), :])
            return Sf, Sb
        return lax.fori_loop(0, n, body, carry, unroll=SCAN_UNROLL)

    def phase_b(q_ref, k_ref, v_ref, o_ref, n, base):
        g = g_ref[...]

        def body(c, carry):
            sl = _chunk(c)
            q, k, v = q_ref[0, sl, :], k_ref[0, sl, :], v_ref[0, sl, :]
            s = (_dot_nt(q, k) * mask).astype(BF16)
            qf = q.astype(F32)
            lhs = jnp.concatenate([s, (qf * qdec_f).astype(BF16), (qf * qdec_b).astype(BF16)], axis=1)
            y = _dot(lhs, jnp.concatenate([v, st[base + c]], axis=0))
            o_ref[0, sl, :] = _rms(y, g).astype(o_ref.dtype)
            return carry
        lax.fori_loop(0, n, body, 0, unroll=2 * SCAN_UNROLL)

    zero = jnp.zeros((dk, dv), F32)
    carry = phase_a(kc_ref, vc_ref, nc, 0, (zero, zero))
    phase_a(kl_ref, vl_ref, nl, nc, carry)
    phase_b(qc_ref, kc_ref, vc_ref, oc_ref, nc, 0)
    phase_b(ql_ref, kl_ref, vl_ref, ol_ref, nl, nc)


def _retention(log_gamma, qk_c, pv_c, qk_l, pv_l, g, v_blk, batch):
    tc, tl = qk_c.shape[1], qk_l.shape[1]
    dk = qk_c.shape[2] // (2 * H_R)
    dv = g.shape[0] // H_R
    qs = lambda t: pl.BlockSpec((1, t, dk), lambda b, h: (b, 0, h))
    ks = lambda t: pl.BlockSpec((1, t, dk), lambda b, h: (b, 0, H_R + h))
    vs = lambda t: pl.BlockSpec((1, t, dv), lambda b, h: (b, 0, v_blk + h))
    os_ = lambda t: pl.BlockSpec((1, t, dv), lambda b, h: (b, 0, h))
    return pl.pallas_call(
        _retention_kernel,
        grid=(batch, H_R),
        in_specs=[pl.BlockSpec(memory_space=pltpu.SMEM),
                  qs(tc), ks(tc), vs(tc), qs(tl), ks(tl), vs(tl),
                  pl.BlockSpec((1, dv), lambda b, h: (0, h))],
        out_specs=[os_(tc), os_(tl)],
        out_shape=[jax.ShapeDtypeStruct((batch, tc, H_R * dv), BF16),
                   jax.ShapeDtypeStruct((batch, tl, H_R * dv), BF16)],
        scratch_shapes=[pltpu.VMEM(((tc + tl) // CHUNK, 2 * dk, dv), BF16)],
        compiler_params=_params("parallel", "arbitrary"),
        name="retention",
    )(log_gamma, qk_c, qk_c, pv_c, qk_l, qk_l, pv_l, g.reshape(1, -1))


def _lane_scan(x, reverse, op, identity):
    n = x.shape[-1]
    lane = lax.broadcasted_iota(jnp.int32, x.shape, 1)
    s = 1
    while s < n:
        if reverse:
            x = op(x, jnp.where(lane < n - s, pltpu.roll(x, n - s, axis=1), identity))
        else:
            x = op(x, jnp.where(lane >= s, pltpu.roll(x, s, axis=1), identity))
        s *= 2
    return x


_B, _R, _CM, _W, _BE, _ML = range(6)


def _col_of_row(r):
    return jnp.broadcast_to(r, (r.shape[1], r.shape[1])).T


def _mlstm_kernel(gb_ref, qc_ref, kc_ref, vc_ref, gic_ref, gfc_ref, gibc_ref, gfbc_ref,
                  ql_ref, kl_ref, vl_ref, gil_ref, gfl_ref, gibl_ref, gfbl_ref, g_ref,
                  oc_ref, ol_ref, cst, mst, kt, sfc, sbc, sfl, sbl):
    h = pl.program_id(1)
    L = CHUNK
    dk = qc_ref.shape[-1]
    dv = vc_ref.shape[-1]
    assert dk == L
    nc = qc_ref.shape[1] // L
    nl = ql_ref.shape[1] // L
    row = lax.broadcasted_iota(jnp.int32, (L, L), 0)
    col = lax.broadcasted_iota(jnp.int32, (L, L), 1)
    tri = (col <= row, col >= row)
    end_lane = (L - 1, 0)

    ones_col = jnp.ones((L, L), BF16)

    def v_aug(v_ref, c):
        return jnp.concatenate([v_ref[0, _chunk(c), :], ones_col], axis=1)

    def prep(gi_ref, gf_ref, scr, d):
        rev = d == 1
        ig = (gi_ref[0, 0] + gb_ref[2 * d, h]) * LOG2E
        lf = _log_sigmoid(gf_ref[0, 0] + gb_ref[2 * d + 1, h]) * LOG2E
        b = _lane_scan(lf, rev, jnp.add, 0.0)
        r = ig - b
        lane = lax.broadcasted_iota(jnp.int32, b.shape, 1)
        be = jnp.broadcast_to(
            jnp.sum(jnp.where(lane == end_lane[d], b, 0.0), axis=-1, keepdims=True), b.shape)
        loc = be + r
        m_loc = jnp.broadcast_to(jnp.max(loc, axis=-1, keepdims=True), b.shape)
        scr[_B] = b
        scr[_R] = r
        scr[_CM] = _lane_scan(r, rev, jnp.maximum, -jnp.inf)
        scr[_W] = jnp.exp2(loc - m_loc)
        scr[_BE] = be
        scr[_ML] = m_loc

    prep(gic_ref, gfc_ref, sfc, 0)
    prep(gibc_ref, gfbc_ref, sbc, 1)
    prep(gil_ref, gfl_ref, sfl, 0)
    prep(gibl_ref, gfbl_ref, sbl, 1)

    def transpose_k(k_ref, n, base):
        def body(c, carry):
            kt[base + c] = k_ref[0, _chunk(c), :].T
            return carry
        lax.fori_loop(0, n, body, 0, unroll=SCAN_UNROLL)

    transpose_k(kc_ref, nc, 0)
    transpose_k(kl_ref, nl, nc)

    def phase_a(v_ref, scr, n, base, carry):
        def step(d, c, state):
            C, m = state
            s = scr[d]
            cst[d, base + c] = C.astype(BF16)
            mst[d, base + c] = m
            row_of = lambda idx: s[idx, pl.ds(c, 1), :]
            kw = (kt[base + c].astype(F32) * row_of(_W)).astype(BF16)
            dC = _dot(kw, v_aug(v_ref, c))
            be, m_loc = row_of(_BE), row_of(_ML)
            m_new = jnp.maximum(be + m, m_loc)
            f_old = jnp.exp2(be + m - m_new)
            f_new = jnp.exp2(m_loc - m_new)
            wide = lambda f: jnp.concatenate([f] * (C.shape[1] // L), axis=1)
            return wide(f_old) * C + wide(f_new) * dC, m_new

        def body(t, carry):
            return step(0, t, carry[0]), step(1, n - 1 - t, carry[1])
        return lax.fori_loop(0, n, body, carry, unroll=SCAN_UNROLL)

    def phase_b(q_ref, k_ref, v_ref, o_ref, scr, n, base):
        g = g_ref[...]
        ones_dv = jnp.ones((dv, L), BF16)

        def body(c, carry):
            sl = _chunk(c)
            q, k = q_ref[0, sl, :], k_ref[0, sl, :]
            va = v_aug(v_ref, c)
            qk = _dot_nt(q, k)
            qf = q.astype(F32)
            y = None
            for d in (0, 1):
                s = scr[d]
                row_of = lambda idx: s[idx, pl.ds(c, 1), :]
                m_prev = mst[d, base + c]
                m_t = row_of(_B) + jnp.maximum(row_of(_CM), m_prev)
                U = _col_of_row(row_of(_B) - m_t)
                NM = _col_of_row(-m_t)
                p = jnp.exp2(jnp.where(tri[d], U + row_of(_R), -jnp.inf))
                a = jnp.exp2(U + m_prev)
                lhs = jnp.concatenate([(qk * p).astype(BF16), (a * qf).astype(BF16)], axis=1)
                rhs = jnp.concatenate([va, cst[d, base + c]], axis=0)
                res = _dot(lhs, rhs)
                inv = 1.0 / jnp.maximum(jnp.abs(res[:, dv:]), jnp.exp2(NM))
                hval = res[:, :dv] * jnp.concatenate([inv] * (dv // L), axis=1)
                y = hval if y is None else y + hval
            ms = _dot((y * y).astype(BF16), ones_dv) * (1.0 / dv)
            scale = lax.rsqrt(ms + EPS)
            o_ref[0, sl, :] = (y * jnp.concatenate([scale] * (dv // L), axis=1) * g).astype(o_ref.dtype)
            return carry
        lax.fori_loop(0, n, body, 0, unroll=2 * SCAN_UNROLL)

    zero = (jnp.zeros((dk, dv + L), F32), jnp.zeros((1, L), F32))
    scr_c = (sfc, sbc)
    scr_l = (sfl, sbl)
    carry = phase_a(vc_ref, scr_c, nc, 0, (zero, zero))
    phase_a(vl_ref, scr_l, nl, nc, carry)
    phase_b(qc_ref, kc_ref, vc_ref, oc_ref, scr_c, nc, 0)
    phase_b(ql_ref, kl_ref, vl_ref, ol_ref, scr_l, nl, nc)


def _mlstm(gate_b, q_c, k_c, pv_c, gates_c, q_l, k_l, pv_l, gates_l, g, v_blk, batch):
    tc, tl = q_c.shape[1], q_l.shape[1]
    dk = q_c.shape[2] // H_M
    dv = g.shape[0] // H_M
    qs = lambda t: pl.BlockSpec((1, t, dk), lambda b, h: (b, 0, h))
    ks = qs
    vs = lambda t: pl.BlockSpec((1, t, dv), lambda b, h: (b, 0, v_blk + h))
    gs = lambda t, k: pl.BlockSpec((1, 1, t // CHUNK, CHUNK), lambda b, h: (b, k * H_M + h, 0, 0))
    os_ = lambda t: pl.BlockSpec((1, t, dv), lambda b, h: (b, 0, h))

    def stream(t):
        return [qs(t), ks(t), vs(t), gs(t, 0), gs(t, 1), gs(t, 2), gs(t, 3)]

    n_slots = (tc + tl) // CHUNK
    rows = lambda t: pltpu.VMEM((6, t // CHUNK, CHUNK), F32)
    return pl.pallas_call(
        _mlstm_kernel,
        grid=(batch, H_M),
        in_specs=[pl.BlockSpec(memory_space=pltpu.SMEM)] + stream(tc) + stream(tl)
                 + [pl.BlockSpec((1, dv), lambda b, h: (0, h))],
        out_specs=[os_(tc), os_(tl)],
        out_shape=[jax.ShapeDtypeStruct((batch, tc, H_M * dv), BF16),
                   jax.ShapeDtypeStruct((batch, tl, H_M * dv), BF16)],
        scratch_shapes=[pltpu.VMEM((2, n_slots, dk, dv + CHUNK), BF16),
                        pltpu.VMEM((2, n_slots, 1, CHUNK), F32),
                        pltpu.VMEM((n_slots, dk, CHUNK), BF16),
                        rows(tc), rows(tc), rows(tl), rows(tl)],
        compiler_params=_params("parallel", "arbitrary"),
        name="mlstm",
    )(gate_b, q_c, k_c, pv_c, gates_c, gates_c, gates_c, gates_c,
      q_l, k_l, pv_l, gates_l, gates_l, gates_l, gates_l, g.reshape(1, -1))


def _merge_kernel(yr_ref, rg_ref, ym_ref, mo_ref, gr_ref, gm_ref, wr_ref, wm_ref, o_ref):
    a = (yr_ref[...] * _silu(rg_ref[...].astype(F32)).astype(BF16))
    b = (ym_ref[...] * _sigmoid(mo_ref[...].astype(F32)).astype(BF16))
    y = (_sigmoid(gr_ref[...].astype(F32)) * _dot(a, wr_ref[...])
         + _sigmoid(gm_ref[...].astype(F32)) * _dot(b, wm_ref[...]))
    o_ref[...] = y.astype(o_ref.dtype)


def _merge(yr, ym, pa, pb, wr, wm, layer):
    m, vw = yr.shape
    d = wr.shape[2]
    bm = _tile(m, BM_ROW, 16)
    col = lambda k: pl.BlockSpec((bm, vw), lambda i: (i, k))
    wspec = pl.BlockSpec((None, vw, d), lambda i: (layer, 0, 0), pipeline_mode=pl.Buffered(1))
    return pl.pallas_call(
        _merge_kernel,
        grid=(m // bm,),
        in_specs=[col(0), col(1), col(0), col(3), col(0), col(1), wspec, wspec],
        out_specs=pl.BlockSpec((bm, d), lambda i: (i, 0)),
        out_shape=jax.ShapeDtypeStruct((m, d), BF16),
        compiler_params=_params("parallel"),
        name="merge",
    )(yr, pa, ym, pa, pb, pb, wr, wm)


def _resid_kernel(a_ref, w_ref, x_ref, go_ref, gate_ref, gn_ref, sh_ref, sc_ref, xo_ref, ho_ref):
    for r, rb in _row_parts(a_ref.shape[0], BM_ROW):
        rs = slice(r, r + rb)
        out = _dot(a_ref[rs, :], w_ref[...])
        xn = x_ref[rs, :] + gate_ref[0] * _rms(out, go_ref[...])
        xo_ref[rs, :] = xn
        ho_ref[rs, :] = _normmod(xn, gn_ref[...], sh_ref[0], sc_ref[0]).astype(ho_ref.dtype)


def _resid_last_kernel(a_ref, w_ref, x_ref, go_ref, gate_ref, xo_ref):
    for r, rb in _row_parts(a_ref.shape[0], BM_ROW):
        rs = slice(r, r + rb)
        out = _dot(a_ref[rs, :], w_ref[...])
        xo_ref[rs, :] = x_ref[rs, :] + gate_ref[0] * _rms(out, go_ref[...])


def _resid(a, w, layer, x, g_out, gate, rows_per_mod, nxt=None, name="resid"):
    m, k = a.shape
    d = w.shape[2]
    bm = _tile(rows_per_mod, 2 * BM_ROW, 16)
    tiles = 2 * bm * (2 * k + 4 * d + 4 * d + 2 * d) + bm * 4 * d
    if 2 * k * d + tiles > V7X_VMEM_LIMIT:
        bm = _tile(rows_per_mod, BM_ROW, 16)
    per = rows_per_mod // bm
    row = lambda width: pl.BlockSpec((bm, width), lambda i: (i, 0))
    vec = pl.BlockSpec((1, 1, d), lambda i: (i // per, 0, 0))
    gvec = pl.BlockSpec((1, d), lambda i: (0, 0))
    in_specs = [row(k),
                pl.BlockSpec((None, k, d), lambda i: (layer, 0, 0), pipeline_mode=pl.Buffered(1)),
                row(d), gvec, vec]
    args = [a, w, x, g_out.reshape(1, d), gate]
    if nxt is None:
        return pl.pallas_call(
            _resid_last_kernel, grid=(m // bm,), in_specs=in_specs, out_specs=row(d),
            out_shape=jax.ShapeDtypeStruct((m, d), F32),
            compiler_params=_params("parallel"), name=name,
        )(*args)
    g_next, shift, scale = nxt
    return pl.pallas_call(
        _resid_kernel, grid=(m // bm,),
        in_specs=in_specs + [gvec, vec, vec],
        out_specs=[row(d), row(d)],
        out_shape=[jax.ShapeDtypeStruct((m, d), F32), jax.ShapeDtypeStruct((m, d), BF16)],
        compiler_params=_params("parallel"), name=name,
    )(*args, g_next.reshape(1, d), shift, scale)


def _rope_tables(seq, dk, q_scale):
    quarter = dk // 4
    t = jnp.arange(seq)
    rows = (t // GRID_W).astype(F32)
    cols = (t % GRID_W).astype(F32)
    freqs = ROPE_BASE ** (-jnp.arange(quarter, dtype=F32) / quarter)
    ar = rows[:, None] * freqs[None, :]
    ac = cols[:, None] * freqs[None, :]
    cos = jnp.concatenate([jnp.cos(ar), jnp.cos(ar), jnp.cos(ac), jnp.cos(ac)], axis=-1)
    sin = jnp.concatenate([-jnp.sin(ar), jnp.sin(ar), -jnp.sin(ac), jnp.sin(ac)], axis=-1)
    scale = jnp.array([q_scale, 1.0], F32)[:, None, None]
    return cos[None] * scale, sin[None] * scale


def _gates_t_kernel(wt_ref, h_ref, o_ref):
    o_ref[0] = _dot_nt(wt_ref[...], h_ref[...])


def _gates_t(h, wt, layer, batch, seq, name):
    m, d = h.shape
    ng = wt.shape[1]
    bm = _tile(seq, BM_PROJ, 128)
    per = seq // bm
    out = pl.pallas_call(
        _gates_t_kernel,
        grid=(m // bm,),
        in_specs=[pl.BlockSpec((None, ng, d), lambda i: (layer, 0, 0)),
                  pl.BlockSpec((bm, d), lambda i: (i, 0))],
        out_specs=pl.BlockSpec((1, ng, bm), lambda i: (i // per, 0, i % per)),
        out_shape=jax.ShapeDtypeStruct((batch, ng, seq), F32),
        compiler_params=_params("parallel"),
        name=name,
    )(wt, h)
    return out.reshape(batch, ng, seq // CHUNK, CHUNK)


def kernel(x, c, ctx, c_ctx, w_ada, b_ada, norm_w, w_in, mlstm_conv_w, mlstm_conv_b, mlstm_gate_b,
           ret_decay_exp, head_norm_w, w_ret_out, w_mlstm_out, w_o, w_up, ffn_conv_w, ffn_conv_b,
           w_down):
    batch, seq, d = x.shape
    ctx_len = ctx.shape[1]
    depth = w_ada.shape[0]
    vw = w_ret_out.shape[1]
    r_qk = (w_in.shape[2] - 4 * vw - 4 * H_M - 2 * d) // 4
    r_dk = r_qk // H_R
    m_dk = r_qk // H_M
    assert d == vw, "column-block addressing of the projection output assumes D_MODEL == VW"

    o_rv = 2 * r_qk
    o_mq = o_rv + 2 * vw
    o_mv = o_mq + 2 * r_qk
    o_mg = o_mv + 2 * vw
    o_merge = o_mg + 4 * H_M

    pad = (-(batch + 1)) % 8
    cvec = jnp.concatenate([c, c_ctx[None], jnp.zeros((pad, d), F32)], axis=0)
    mod = _ada(cvec, w_ada, b_ada)

    def mod_l(l, k):
        return mod[l, :batch, k * d:(k + 1) * d].reshape(batch, 1, d)

    def mod_c(l, k):
        return mod[l, batch:batch + 1, k * d:(k + 1) * d].reshape(1, 1, d)

    rope = _rope_tables(seq, r_dk, r_dk ** -0.5)
    qscale_r = jnp.concatenate([jnp.full((r_qk,), r_dk ** -0.5, F32), jnp.ones((r_qk,), F32)])
    log_gamma = jnp.log1p(-jnp.exp2(-ret_decay_exp.astype(F32)))

    ml, mc = batch * seq, batch * ctx_len
    xl = x.reshape(ml, d)
    xc = ctx.reshape(mc, d)
    hl = _normmod_call(xl, norm_w[0, 0], mod_l(0, 0), mod_l(0, 1), seq)
    hc = _normmod_call(xc, norm_w[0, 0], mod_c(0, 0), mod_c(0, 1), mc)

    w_in_b = jnp.swapaxes(w_in, 1, 2).astype(BF16)
    w_mg_b = w_in_b[:, o_merge:, :]
    w_gate_t = w_in_b[:, o_mg:o_merge, :]
    wr_b, wm_b, wo_b, wu_b, wd_b = (w.astype(BF16) for w in (w_ret_out, w_mlstm_out, w_o, w_up, w_down))
    d_ff = w_down.shape[1]

    for l in range(depth):
        last = l == depth - 1
        cw, cb = mlstm_conv_w[l], mlstm_conv_b[l]

        def project(h, seq_len, rope_tabs, nm):
            pa = _proj(h, w_in_b, l, ((o_rv, 2 * vw), (o_mv, 2 * vw)), name=nm + "_plain")
            pb = _proj(h, w_mg_b, l, ((0, 2 * d),), name=nm + "_mgate")
            if rope_tabs is None:
                rqk = _proj(h, w_in_b, l, ((0, o_rv),), colscale=qscale_r, name=nm + "_rqk")
            else:
                rqk = _proj(h, w_in_b, l, ((0, o_rv),), rope=rope_tabs, seq=seq_len, name=nm + "_rqk")
            mq, mk = _convproj(h, w_in_b, l, o_mq, r_qk, cw, cb, seq_len, scale_a=m_dk ** -0.5,
                               transposed=True, name=nm + "_mqk")
            b = h.shape[0] // seq_len
            return (pa, pb, rqk.reshape(b, seq_len, -1),
                    (mq.reshape(b, seq_len, -1), mk.reshape(b, seq_len, -1)),
                    _gates_t(h, w_gate_t, l, b, seq_len, nm + "_gates"))

        pa_c, pb_c, rqk_c, mqk_c, g_c = project(hc, ctx_len, None, "ctx")
        pa_l, pb_l, rqk_l, mqk_l, g_l = project(hl, seq, rope, "lat")
        pv_c = pa_c.reshape(batch, ctx_len, -1)
        pv_l = pa_l.reshape(batch, seq, -1)

        yr_c, yr_l = _retention(log_gamma[l], rqk_c, pv_c, rqk_l, pv_l, head_norm_w[l, 0],
                                0 * H_R, batch)
        ym_c, ym_l = _mlstm(mlstm_gate_b[l], *mqk_c, pv_c, g_c, *mqk_l, pv_l, g_l, head_norm_w[l, 1],
                            2 * H_M, batch)

        def tail(yr, ym, pa, pb, xres, mod_k, rows_per_mod, seq_len, nm):
            y = _merge(yr.reshape(-1, vw), ym.reshape(-1, vw), pa, pb, wr_b, wm_b, l)
            x1, h2 = _resid(y, wo_b, l, xres, norm_w[l, 1], mod_k(l, 2), rows_per_mod,
                            nxt=(norm_w[l, 2], mod_k(l, 3), mod_k(l, 4)), name=nm + "_o")
            act = _convproj(h2, wu_b, l, 0, d_ff, ffn_conv_w[l], ffn_conv_b[l], seq_len, name=nm + "_ffn")
            if last:
                return _resid(act, wd_b, l, x1, norm_w[l, 3], mod_k(l, 5), rows_per_mod,
                              name=nm + "_down"), None
            return _resid(act, wd_b, l, x1, norm_w[l, 3], mod_k(l, 5), rows_per_mod,
                          nxt=(norm_w[l + 1, 0], mod_k(l + 1, 0), mod_k(l + 1, 1)), name=nm + "_down")

        xl, hl = tail(yr_l, ym_l, pa_l, pb_l, xl, mod_l, seq, seq, "lat")
        if not last:
            xc, hc = tail(yr_c, ym_c, pa_c, pb_c, xc, mod_c, mc, ctx_len, "ctx")

    return xl.reshape(batch, seq, d)
```

```python
import functools

import jax
import jax.numpy as jnp
from jax import lax
from jax.experimental import pallas as pl
from jax.experimental.pallas import tpu as pltpu

F32 = jnp.float32
BF16 = jnp.bfloat16

GRID_W = 64
CHUNK = 128
H_R = 8
H_M = 8
N_MOD = 6
CONV_W = 3
ROPE_BASE = 10000.0
EPS = 1e-6
LOG2E = 1.4426950408889634
HALO = 16
ROWS_PART = 512
V7X_VMEM_LIMIT = 56 * 1024 * 1024

BM_PROJ = 1024
BN_PROJ = 2048
BM_CONV = 1024
BN_CONV = 512
BM_ROW = 256
BM_NORM = 512
BN_ADA = 1024
SCAN_UNROLL = 4


def _tile(n, pref, quantum):
    if n <= pref:
        return n
    t = (pref // quantum) * quantum
    while t > quantum and n % t:
        t -= quantum
    assert n % t == 0, (n, pref, quantum)
    return t


def _params(*sem):
    return pltpu.CompilerParams(dimension_semantics=sem, vmem_limit_bytes=V7X_VMEM_LIMIT)


def _sigmoid(x):
    return 1.0 / (1.0 + jnp.exp(-x))


def _silu(x):
    return x * _sigmoid(x)


def _log_sigmoid(x):
    return jnp.minimum(x, 0.0) - jnp.log1p(jnp.exp(-jnp.abs(x)))


def _rms(x, g):
    ms = jnp.mean(x * x, axis=-1, keepdims=True)
    return x * lax.rsqrt(ms + EPS) * g


def _normmod(x, g, shift, scale):
    return _rms(x, g) * (1.0 + scale) + shift


def _dot(a, b):
    return jnp.dot(a, b, preferred_element_type=F32)


def _dot_nt(a, b):
    return lax.dot_general(a, b, (((1,), (1,)), ((), ())), preferred_element_type=F32)


def _dot_tn(a, b):
    return lax.dot_general(a, b, (((0,), (0,)), ((), ())), preferred_element_type=F32)


def _ada_kernel(c_ref, w_ref, b_ref, o_ref):
    s = _silu(c_ref[...]).astype(BF16)
    o_ref[0] = _dot(s, w_ref[0].astype(BF16)) + b_ref[0]


def _ada(cvec, w_ada, b_ada):
    depth, d, n = w_ada.shape
    r = cvec.shape[0]
    bn = _tile(n, BN_ADA, 128)
    return pl.pallas_call(
        _ada_kernel,
        grid=(depth, n // bn),
        in_specs=[pl.BlockSpec((r, d), lambda l, j: (0, 0)),
                  pl.BlockSpec((1, d, bn), lambda l, j: (l, 0, j)),
                  pl.BlockSpec((1, 1, bn), lambda l, j: (l, 0, j))],
        out_specs=pl.BlockSpec((1, r, bn), lambda l, j: (l, 0, j)),
        out_shape=jax.ShapeDtypeStruct((depth, r, n), F32),
        compiler_params=_params("arbitrary", "arbitrary"),
        name="ada",
    )(cvec, w_ada, b_ada.reshape(depth, 1, n))


def _normmod_kernel(x_ref, g_ref, sh_ref, sc_ref, o_ref):
    o_ref[...] = _normmod(x_ref[...], g_ref[...], sh_ref[0], sc_ref[0]).astype(o_ref.dtype)


def _normmod_call(x, g, shift, scale, rows_per_mod):
    m, d = x.shape
    bm = _tile(rows_per_mod, BM_NORM, 16)
    per = rows_per_mod // bm
    vec = pl.BlockSpec((1, 1, d), lambda i: (i // per, 0, 0))
    return pl.pallas_call(
        _normmod_kernel,
        grid=(m // bm,),
        in_specs=[pl.BlockSpec((bm, d), lambda i: (i, 0)),
                  pl.BlockSpec((1, d), lambda i: (0, 0)), vec, vec],
        out_specs=pl.BlockSpec((bm, d), lambda i: (i, 0)),
        out_shape=jax.ShapeDtypeStruct((m, d), BF16),
        compiler_params=_params("parallel"),
        name="normmod",
    )(x, g.reshape(1, d), shift, scale)


def _proj_kernel(h_ref, w_ref, o_ref):
    o_ref[...] = _dot(h_ref[...], w_ref[...]).astype(o_ref.dtype)


def _proj_scale_kernel(h_ref, w_ref, s_ref, o_ref):
    o_ref[...] = (_dot(h_ref[...], w_ref[...]) * s_ref[...]).astype(o_ref.dtype)


def _swap32(r):
    lane = lax.broadcasted_iota(jnp.int32, r.shape, 1)
    return jnp.where((lane & 32) == 0, pltpu.roll(r, 96, axis=1), pltpu.roll(r, 32, axis=1))


def _row_parts(bm, rows):
    rb = rows if bm % rows == 0 else bm
    return [(r, rb) for r in range(0, bm, rb)]


def _proj_rope_kernel(h_ref, w_ref, c_ref, s_ref, o_ref):
    for r0, rb in _row_parts(h_ref.shape[0], ROWS_PART):
        rs = slice(r0, r0 + rb)
        res = _dot(h_ref[rs, :], w_ref[...])
        cos, sin = c_ref[0, rs, :], s_ref[0, rs, :]
        for g in range(res.shape[1] // 128):
            r = res[:, g * 128:(g + 1) * 128]
            o_ref[rs, g * 128:(g + 1) * 128] = (r * cos + _swap32(r) * sin).astype(o_ref.dtype)


def _col_block_map(ranges, bn):
    assert all(first % bn == 0 and cnt % bn == 0 for first, cnt in ranges), (ranges, bn)

    def block(j):
        blk = j + ranges[0][0] // bn
        seen = 0
        for (first, cnt), (nxt, _) in zip(ranges[:-1], ranges[1:]):
            seen += cnt // bn
            blk = blk + jnp.where(j >= seen, (nxt - first - cnt) // bn, 0)
        return blk
    return block


def _proj(h, w, layer, ranges, colscale=None, rope=None, seq=None, name="proj"):
    m, d = h.shape
    n = sum(cnt for _, cnt in ranges)
    bm = _tile(m if seq is None else seq, BM_PROJ, 16)
    bn = _tile(n, BN_PROJ, 128) if rope is None else n // 2
    block = _col_block_map(ranges, bn)
    in_specs = [pl.BlockSpec((bm, d), lambda i, j: (i, 0)),
                pl.BlockSpec((None, d, bn), lambda i, j: (layer, 0, block(j)))]
    args = [h, w]
    kern = _proj_kernel
    if colscale is not None:
        kern = _proj_scale_kernel
        in_specs.append(pl.BlockSpec((1, bn), lambda i, j: (0, j)))
        args.append(colscale.reshape(1, n))
    if rope is not None:
        kern = _proj_rope_kernel
        cos, sin = rope
        assert cos.shape == (n // bn, seq, 128)
        per = seq // bm
        tab = pl.BlockSpec((1, bm, 128), lambda i, j: (j, i % per, 0))
        in_specs += [tab, tab]
        args += [cos, sin]
    return pl.pallas_call(
        kern,
        grid=(m // bm, n // bn),
        in_specs=in_specs,
        out_specs=pl.BlockSpec((bm, bn), lambda i, j: (i, j)),
        out_shape=jax.ShapeDtypeStruct((m, n), BF16),
        compiler_params=_params("parallel", "arbitrary"),
        name=name,
    )(*args)


def _fill_halo(hp_ref, hc_ref, hn_ref, hext, first, last, bm):
    hext[0:HALO, :] = hp_ref[...]
    hext[HALO:HALO + bm, :] = hc_ref[...]
    hext[HALO + bm:, :] = hn_ref[...]

    @pl.when(first)
    def _():
        hext[0:HALO, :] = jnp.zeros((HALO, hext.shape[1]), hext.dtype)

    @pl.when(last)
    def _():
        hext[HALO + bm:, :] = jnp.zeros((HALO, hext.shape[1]), hext.dtype)


def _conv3(u, cw, cb, bm):
    rows = u.shape[0]
    prev = pltpu.roll(u, 1, axis=0)[HALO:HALO + bm]
    cur = u[HALO:HALO + bm]
    nxt = pltpu.roll(u, rows - 1, axis=0)[HALO:HALO + bm]
    return cb + prev * cw[0:1] + cur * cw[1:2] + nxt * cw[2:3]


def _convproj_pair_kernel(hp_ref, hc_ref, hn_ref, wa_ref, wg_ref, cwa_ref, cba_ref, cwg_ref, cbg_ref,
                          oa_ref, og_ref, hext, *, bm, per, scale_a):
    i, j = pl.program_id(0), pl.program_id(1)

    @pl.when(j == 0)
    def _():
        _fill_halo(hp_ref, hc_ref, hn_ref, hext, i % per == 0, i % per == per - 1, bm)

    hx = hext[...]
    a = _conv3(_dot(hx, wa_ref[...]), cwa_ref[...], cba_ref[...], bm)
    oa_ref[...] = (_silu(a) * scale_a).astype(oa_ref.dtype)
    g = _conv3(_dot(hx, wg_ref[...]), cwg_ref[...], cbg_ref[...], bm)
    og_ref[...] = _silu(g).astype(og_ref.dtype)


def _convproj_glu_kernel(hp_ref, hc_ref, hn_ref, wa_ref, wg_ref, cwa_ref, cba_ref, cwg_ref, cbg_ref,
                         o_ref, hext, *, bm, per):
    i, j = pl.program_id(0), pl.program_id(1)

    @pl.when(j == 0)
    def _():
        _fill_halo(hp_ref, hc_ref, hn_ref, hext, i % per == 0, i % per == per - 1, bm)

    hx = hext[...]
    a = _conv3(_dot(hx, wa_ref[...]), cwa_ref[...], cba_ref[...], bm)
    g = _conv3(_dot(hx, wg_ref[...]), cwg_ref[...], cbg_ref[...], bm)
    o_ref[...] = (_silu(a) * g).astype(o_ref.dtype)


def _convproj(h, w, layer, col0, n, cw, cb, seq, scale_a=None, name="convproj"):
    glu = scale_a is None
    m, d = h.shape
    bm = _tile(seq, BM_CONV, HALO)
    bn = _tile(n, BN_CONV, 128)
    assert col0 % bn == 0
    first = col0 // bn
    per = seq // bm
    nb = bm // HALO
    last_blk = m // HALO - 1
    hspecs = [pl.BlockSpec((HALO, d), lambda i, j: (jnp.maximum(i * nb - 1, 0), 0)),
              pl.BlockSpec((bm, d), lambda i, j: (i, 0)),
              pl.BlockSpec((HALO, d), lambda i, j: (jnp.minimum((i + 1) * nb, last_blk), 0))]
    wspec = lambda off: pl.BlockSpec((None, d, bn), lambda i, j: (layer, 0, first + j + off))
    cwspec = lambda off: pl.BlockSpec((CONV_W, bn), lambda i, j: (0, j + off))
    cbspec = lambda off: pl.BlockSpec((1, bn), lambda i, j: (0, j + off))
    cb2 = cb.reshape(1, -1)
    off = n // bn
    if glu:
        kern = functools.partial(_convproj_glu_kernel, bm=bm, per=per)
    else:
        kern = functools.partial(_convproj_pair_kernel, bm=bm, per=per, scale_a=scale_a)
    out_spec = pl.BlockSpec((bm, bn), lambda i, j: (i, j))
    out_shape = jax.ShapeDtypeStruct((m, n), BF16)
    return pl.pallas_call(
        kern,
        grid=(m // bm, n // bn),
        in_specs=hspecs + [wspec(0), wspec(off), cwspec(0), cbspec(0), cwspec(off), cbspec(off)],
        out_specs=out_spec if glu else [out_spec, out_spec],
        out_shape=out_shape if glu else [out_shape, out_shape],
        scratch_shapes=[pltpu.VMEM((bm + 2 * HALO, d), BF16)],
        compiler_params=_params("parallel", "arbitrary"),
        name=name,
    )(h, h, h, w, w, cw, cb2, cw, cb2)


def _chunk(c):
    return pl.ds(pl.multiple_of(c * CHUNK, CHUNK), CHUNK)


def _retention_kernel(lg_ref, qc_ref, kc_ref, vc_ref, ql_ref, kl_ref, vl_ref, g_ref,
                      oc_ref, ol_ref, st):
    h = pl.program_id(1)
    L = CHUNK
    dk = qc_ref.shape[-1]
    dv = vc_ref.shape[-1]
    assert dk == L
    nc = qc_ref.shape[1] // L
    nl = ql_ref.shape[1] // L
    lgf, lgb = lg_ref[0, h], lg_ref[1, h]
    row = lax.broadcasted_iota(jnp.int32, (L, L), 0).astype(F32)
    col = lax.broadcasted_iota(jnp.int32, (L, L), 1).astype(F32)
    rel = row - col
    mask = (jnp.where(rel >= 0, jnp.exp(jnp.maximum(rel, 0.0) * lgf), 0.0)
            + jnp.where(rel <= 0, jnp.exp(jnp.maximum(-rel, 0.0) * lgb), 0.0))
    qdec_f = jnp.exp((row + 1.0) * lgf)
    kdec_f = jnp.exp((L - 1.0 - row) * lgf)
    qdec_b = jnp.exp((L - row) * lgb)
    kdec_b = jnp.exp(row * lgb)
    cdec_f = jnp.exp(jnp.full((1, dv), L, F32) * lgf)
    cdec_b = jnp.exp(jnp.full((1, dv), L, F32) * lgb)

    def phase_a(k_ref, v_ref, n, base, carry):
        def body(t, carry):
            Sf, Sb = carry
            cf, cb = t, n - 1 - t
            st[base + cf, 0:dk, :] = Sf.astype(BF16)
            st[base + cb, dk:, :] = Sb.astype(BF16)
            kf = (k_ref[0, _chunk(cf), :].astype(F32) * kdec_f).astype(BF16)
            kb = (k_ref[0, _chunk(cb), :].astype(F32) * kdec_b).astype(BF16)
            Sf = cdec_f * Sf + _dot_tn(kf, v_ref[0, _chunk(cf), :])
            Sb = cdec_b * Sb + _dot_tn(kb, v_ref[0, _chunk(cb), :])
            return Sf, Sb
        return lax.fori_loop(0, n, body, carry, unroll=SCAN_UNROLL)

    def phase_b(q_ref, k_ref, v_ref, o_ref, n, base):
        g = g_ref[...]

        def body(c, carry):
            sl = _chunk(c)
            q, k, v = q_ref[0, sl, :], k_ref[0, sl, :], v_ref[0, sl, :]
            s = (_dot_nt(q, k) * mask).astype(BF16)
            qf = q.astype(F32)
            lhs = jnp.concatenate([s, (qf * qdec_f).astype(BF16), (qf * qdec_b).astype(BF16)], axis=1)
            y = _dot(lhs, jnp.concatenate([v, st[base + c]], axis=0))
            o_ref[0, sl, :] = _rms(y, g).astype(o_ref.dtype)
            return carry
        lax.fori_loop(0, n, body, 0, unroll=2 * SCAN_UNROLL)

    zero = jnp.zeros((dk, dv), F32)
    carry = phase_a(kc_ref, vc_ref, nc, 0, (zero, zero))
    phase_a(kl_ref, vl_ref, nl, nc, carry)
    phase_b(qc_ref, kc_ref, vc_ref, oc_ref, nc, 0)
    phase_b(ql_ref, kl_ref, vl_ref, ol_ref, nl, nc)


def _retention(log_gamma, qk_c, pv_c, qk_l, pv_l, g, v_blk, batch):
    tc, tl = qk_c.shape[1], qk_l.shape[1]
    dk = qk_c.shape[2] // (2 * H_R)
    dv = g.shape[0] // H_R
    qs = lambda t: pl.BlockSpec((1, t, dk), lambda b, h: (b, 0, h))
    ks = lambda t: pl.BlockSpec((1, t, dk), lambda b, h: (b, 0, H_R + h))
    vs = lambda t: pl.BlockSpec((1, t, dv), lambda b, h: (b, 0, v_blk + h))
    os_ = lambda t: pl.BlockSpec((1, t, dv), lambda b, h: (b, 0, h))
    return pl.pallas_call(
        _retention_kernel,
        grid=(batch, H_R),
        in_specs=[pl.BlockSpec(memory_space=pltpu.SMEM),
                  qs(tc), ks(tc), vs(tc), qs(tl), ks(tl), vs(tl),
                  pl.BlockSpec((1, dv), lambda b, h: (0, h))],
        out_specs=[os_(tc), os_(tl)],
        out_shape=[jax.ShapeDtypeStruct((batch, tc, H_R * dv), BF16),
                   jax.ShapeDtypeStruct((batch, tl, H_R * dv), BF16)],
        scratch_shapes=[pltpu.VMEM(((tc + tl) // CHUNK, 2 * dk, dv), BF16)],
        compiler_params=_params("parallel", "arbitrary"),
        name="retention",
    )(log_gamma, qk_c, qk_c, pv_c, qk_l, qk_l, pv_l, g.reshape(1, -1))


def _lane_scan(x, reverse, op, identity):
    n = x.shape[-1]
    lane = lax.broadcasted_iota(jnp.int32, x.shape, 1)
    s = 1
    while s < n:
        if reverse:
            x = op(x, jnp.where(lane < n - s, pltpu.roll(x, n - s, axis=1), identity))
        else:
            x = op(x, jnp.where(lane >= s, pltpu.roll(x, s, axis=1), identity))
        s *= 2
    return x


_B, _R, _CM, _W, _BE, _ML = range(6)


def _col_of_row(r):
    return jnp.broadcast_to(r, (r.shape[1], r.shape[1])).T


def _mlstm_kernel(gb_ref, qc_ref, kc_ref, vc_ref, gic_ref, gfc_ref, gibc_ref, gfbc_ref,
                  ql_ref, kl_ref, vl_ref, gil_ref, gfl_ref, gibl_ref, gfbl_ref, g_ref,
                  oc_ref, ol_ref, cst, mst, kt, sfc, sbc, sfl, sbl):
    h = pl.program_id(1)
    L = CHUNK
    dk = qc_ref.shape[-1]
    dv = vc_ref.shape[-1]
    assert dk == L
    nc = qc_ref.shape[1] // L
    nl = ql_ref.shape[1] // L
    row = lax.broadcasted_iota(jnp.int32, (L, L), 0)
    col = lax.broadcasted_iota(jnp.int32, (L, L), 1)
    tri = (col <= row, col >= row)
    end_lane = (L - 1, 0)

    ones_col = jnp.ones((L, L), BF16)

    def v_aug(v_ref, c):
        return jnp.concatenate([v_ref[0, _chunk(c), :], ones_col], axis=1)

    def prep(gi_ref, gf_ref, scr, d):
        rev = d == 1
        ig = (gi_ref[0, 0] + gb_ref[2 * d, h]) * LOG2E
        lf = _log_sigmoid(gf_ref[0, 0] + gb_ref[2 * d + 1, h]) * LOG2E
        b = _lane_scan(lf, rev, jnp.add, 0.0)
        r = ig - b
        lane = lax.broadcasted_iota(jnp.int32, b.shape, 1)
        be = jnp.broadcast_to(
            jnp.sum(jnp.where(lane == end_lane[d], b, 0.0), axis=-1, keepdims=True), b.shape)
        loc = be + r
        m_loc = jnp.broadcast_to(jnp.max(loc, axis=-1, keepdims=True), b.shape)
        scr[_B] = b
        scr[_R] = r
        scr[_CM] = _lane_scan(r, rev, jnp.maximum, -jnp.inf)
        scr[_W] = jnp.exp2(loc - m_loc)
        scr[_BE] = be
        scr[_ML] = m_loc

    prep(gic_ref, gfc_ref, sfc, 0)
    prep(gibc_ref, gfbc_ref, sbc, 1)
    prep(gil_ref, gfl_ref, sfl, 0)
    prep(gibl_ref, gfbl_ref, sbl, 1)

    def transpose_k(k_ref, n, base):
        def body(c, carry):
            kt[base + c] = k_ref[0, _chunk(c), :].T
            return carry
        lax.fori_loop(0, n, body, 0, unroll=SCAN_UNROLL)

    transpose_k(kc_ref, nc, 0)
    transpose_k(kl_ref, nl, nc)

    def phase_a(v_ref, scr, n, base, carry):
        def step(d, c, state):
            C, m = state
            s = scr[d]
            cst[d, base + c] = C.astype(BF16)
            mst[d, base + c] = m
            row_of = lambda idx: s[idx, pl.ds(c, 1), :]
            kw = (kt[base + c].astype(F32) * row_of(_W)).astype(BF16)
            dC = _dot(kw, v_aug(v_ref, c))
            be, m_loc = row_of(_BE), row_of(_ML)
            m_new = jnp.maximum(be + m, m_loc)
            f_old = jnp.exp2(be + m - m_new)
            f_new = jnp.exp2(m_loc - m_new)
            wide = lambda f: jnp.concatenate([f] * (C.shape[1] // L), axis=1)
            return wide(f_old) * C + wide(f_new) * dC, m_new

        def body(t, carry):
            return step(0, t, carry[0]), step(1, n - 1 - t, carry[1])
        return lax.fori_loop(0, n, body, carry, unroll=SCAN_UNROLL)

    def phase_b(q_ref, k_ref, v_ref, o_ref, scr, n, base):
        g = g_ref[...]
        ones_dv = jnp.ones((dv, L), BF16)

        def body(c, carry):
            sl = _chunk(c)
            q, k = q_ref[0, sl, :], k_ref[0, sl, :]
            va = v_aug(v_ref, c)
            qk = _dot_nt(q, k)
            qf = q.astype(F32)
            y = None
            for d in (0, 1):
                s = scr[d]
                row_of = lambda idx: s[idx, pl.ds(c, 1), :]
                m_prev = mst[d, base + c]
                m_t = row_of(_B) + jnp.maximum(row_of(_CM), m_prev)
                U = _col_of_row(row_of(_B) - m_t)
                NM = _col_of_row(-m_t)
                p = jnp.exp2(jnp.where(tri[d], U + row_of(_R), -jnp.inf))
                a = jnp.exp2(U + m_prev)
                lhs = jnp.concatenate([(qk * p).astype(BF16), (a * qf).astype(BF16)], axis=1)
                rhs = jnp.concatenate([va, cst[d, base + c]], axis=0)
                res = _dot(lhs, rhs)
                inv = 1.0 / jnp.maximum(jnp.abs(res[:, dv:]), jnp.exp2(NM))
                hval = res[:, :dv] * jnp.concatenate([inv] * (dv // L), axis=1)
                y = hval if y is None else y + hval
            ms = _dot((y * y).astype(BF16), ones_dv) * (1.0 / dv)
            scale = lax.rsqrt(ms + EPS)
            o_ref[0, sl, :] = (y * jnp.concatenate([scale] * (dv // L), axis=1) * g).astype(o_ref.dtype)
            return carry
        lax.fori_loop(0, n, body, 0, unroll=4 * SCAN_UNROLL)

    zero = (jnp.zeros((dk, dv + L), F32), jnp.zeros((1, L), F32))
    scr_c = (sfc, sbc)
    scr_l = (sfl, sbl)
    carry = phase_a(vc_ref, scr_c, nc, 0, (zero, zero))
    phase_a(vl_ref, scr_l, nl, nc, carry)
    phase_b(qc_ref, kc_ref, vc_ref, oc_ref, scr_c, nc, 0)
    phase_b(ql_ref, kl_ref, vl_ref, ol_ref, scr_l, nl, nc)


def _mlstm(gate_b, q_c, k_c, pv_c, gates_c, q_l, k_l, pv_l, gates_l, g, v_blk, batch):
    tc, tl = q_c.shape[1], q_l.shape[1]
    dk = q_c.shape[2] // H_M
    dv = g.shape[0] // H_M
    qs = lambda t: pl.BlockSpec((1, t, dk), lambda b, h: (b, 0, h))
    ks = qs
    vs = lambda t: pl.BlockSpec((1, t, dv), lambda b, h: (b, 0, v_blk + h))
    gs = lambda t, k: pl.BlockSpec((1, 1, t // CHUNK, CHUNK), lambda b, h: (b, k * H_M + h, 0, 0))
    os_ = lambda t: pl.BlockSpec((1, t, dv), lambda b, h: (b, 0, h))

    def stream(t):
        return [qs(t), ks(t), vs(t), gs(t, 0), gs(t, 1), gs(t, 2), gs(t, 3)]

    n_slots = (tc + tl) // CHUNK
    rows = lambda t: pltpu.VMEM((6, t // CHUNK, CHUNK), F32)
    return pl.pallas_call(
        _mlstm_kernel,
        grid=(batch, H_M),
        in_specs=[pl.BlockSpec(memory_space=pltpu.SMEM)] + stream(tc) + stream(tl)
                 + [pl.BlockSpec((1, dv), lambda b, h: (0, h))],
        out_specs=[os_(tc), os_(tl)],
        out_shape=[jax.ShapeDtypeStruct((batch, tc, H_M * dv), BF16),
                   jax.ShapeDtypeStruct((batch, tl, H_M * dv), BF16)],
        scratch_shapes=[pltpu.VMEM((2, n_slots, dk, dv + CHUNK), BF16),
                        pltpu.VMEM((2, n_slots, 1, CHUNK), F32),
                        pltpu.VMEM((n_slots, dk, CHUNK), BF16),
                        rows(tc), rows(tc), rows(tl), rows(tl)],
        compiler_params=_params("parallel", "arbitrary"),
        name="mlstm",
    )(gate_b, q_c, k_c, pv_c, gates_c, gates_c, gates_c, gates_c,
      q_l, k_l, pv_l, gates_l, gates_l, gates_l, gates_l, g.reshape(1, -1))


def _merge_kernel(yr_ref, rg_ref, ym_ref, mo_ref, gr_ref, gm_ref, wr_ref, wm_ref, o_ref):
    a = (yr_ref[...] * _silu(rg_ref[...].astype(F32)).astype(BF16))
    b = (ym_ref[...] * _sigmoid(mo_ref[...].astype(F32)).astype(BF16))
    y = (_sigmoid(gr_ref[...].astype(F32)) * _dot(a, wr_ref[...])
         + _sigmoid(gm_ref[...].astype(F32)) * _dot(b, wm_ref[...]))
    o_ref[...] = y.astype(o_ref.dtype)


def _merge(yr, ym, pa, pb, wr, wm, layer):
    m, vw = yr.shape
    d = wr.shape[2]
    bm = _tile(m, BM_ROW, 16)
    col = lambda k: pl.BlockSpec((bm, vw), lambda i: (i, k))
    wspec = pl.BlockSpec((None, vw, d), lambda i: (layer, 0, 0), pipeline_mode=pl.Buffered(1))
    return pl.pallas_call(
        _merge_kernel,
        grid=(m // bm,),
        in_specs=[col(0), col(1), col(0), col(3), col(0), col(1), wspec, wspec],
        out_specs=pl.BlockSpec((bm, d), lambda i: (i, 0)),
        out_shape=jax.ShapeDtypeStruct((m, d), BF16),
        compiler_params=_params("parallel"),
        name="merge",
    )(yr, pa, ym, pa, pb, pb, wr, wm)


def _resid_kernel(a_ref, w_ref, x_ref, go_ref, gate_ref, gn_ref, sh_ref, sc_ref, xo_ref, ho_ref):
    for r, rb in _row_parts(a_ref.shape[0], BM_ROW):
        rs = slice(r, r + rb)
        out = _dot(a_ref[rs, :], w_ref[...])
        xn = x_ref[rs, :] + gate_ref[0] * _rms(out, go_ref[...])
        xo_ref[rs, :] = xn
        ho_ref[rs, :] = _normmod(xn, gn_ref[...], sh_ref[0], sc_ref[0]).astype(ho_ref.dtype)


def _resid_last_kernel(a_ref, w_ref, x_ref, go_ref, gate_ref, xo_ref):
    for r, rb in _row_parts(a_ref.shape[0], BM_ROW):
        rs = slice(r, r + rb)
        out = _dot(a_ref[rs, :], w_ref[...])
        xo_ref[rs, :] = x_ref[rs, :] + gate_ref[0] * _rms(out, go_ref[...])


def _resid(a, w, layer, x, g_out, gate, rows_per_mod, nxt=None, name="resid"):
    m, k = a.shape
    d = w.shape[2]
    bm = _tile(rows_per_mod, 2 * BM_ROW, 16)
    tiles = 2 * bm * (2 * k + 4 * d + 4 * d + 2 * d) + bm * 4 * d
    if 2 * k * d + tiles > V7X_VMEM_LIMIT:
        bm = _tile(rows_per_mod, BM_ROW, 16)
    per = rows_per_mod // bm
    row = lambda width: pl.BlockSpec((bm, width), lambda i: (i, 0))
    vec = pl.BlockSpec((1, 1, d), lambda i: (i // per, 0, 0))
    gvec = pl.BlockSpec((1, d), lambda i: (0, 0))
    in_specs = [row(k),
                pl.BlockSpec((None, k, d), lambda i: (layer, 0, 0), pipeline_mode=pl.Buffered(1)),
                row(d), gvec, vec]
    args = [a, w, x, g_out.reshape(1, d), gate]
    if nxt is None:
        return pl.pallas_call(
            _resid_last_kernel, grid=(m // bm,), in_specs=in_specs, out_specs=row(d),
            out_shape=jax.ShapeDtypeStruct((m, d), F32),
            compiler_params=_params("parallel"), name=name,
        )(*args)
    g_next, shift, scale = nxt
    return pl.pallas_call(
        _resid_kernel, grid=(m // bm,),
        in_specs=in_specs + [gvec, vec, vec],
        out_specs=[row(d), row(d)],
        out_shape=[jax.ShapeDtypeStruct((m, d), F32), jax.ShapeDtypeStruct((m, d), BF16)],
        compiler_params=_params("parallel"), name=name,
    )(*args, g_next.reshape(1, d), shift, scale)


def _rope_tables(seq, dk, q_scale):
    quarter = dk // 4
    t = jnp.arange(seq)
    rows = (t // GRID_W).astype(F32)
    cols = (t % GRID_W).astype(F32)
    freqs = ROPE_BASE ** (-jnp.arange(quarter, dtype=F32) / quarter)
    ar = rows[:, None] * freqs[None, :]
    ac = cols[:, None] * freqs[None, :]
    cos = jnp.concatenate([jnp.cos(ar), jnp.cos(ar), jnp.cos(ac), jnp.cos(ac)], axis=-1)
    sin = jnp.concatenate([-jnp.sin(ar), jnp.sin(ar), -jnp.sin(ac), jnp.sin(ac)], axis=-1)
    scale = jnp.array([q_scale, 1.0], F32)[:, None, None]
    return cos[None] * scale, sin[None] * scale


def _gates_t_kernel(wt_ref, h_ref, o_ref):
    o_ref[0] = _dot_nt(wt_ref[...], h_ref[...])


def _gates_t(h, wt, layer, batch, seq, name):
    m, d = h.shape
    ng = wt.shape[1]
    bm = _tile(seq, BM_PROJ, 128)
    per = seq // bm
    out = pl.pallas_call(
        _gates_t_kernel,
        grid=(m // bm,),
        in_specs=[pl.BlockSpec((None, ng, d), lambda i: (layer, 0, 0)),
                  pl.BlockSpec((bm, d), lambda i: (i, 0))],
        out_specs=pl.BlockSpec((1, ng, bm), lambda i: (i // per, 0, i % per)),
        out_shape=jax.ShapeDtypeStruct((batch, ng, seq), F32),
        compiler_params=_params("parallel"),
        name=name,
    )(wt, h)
    return out.reshape(batch, ng, seq // CHUNK, CHUNK)


def kernel(x, c, ctx, c_ctx, w_ada, b_ada, norm_w, w_in, mlstm_conv_w, mlstm_conv_b, mlstm_gate_b,
           ret_decay_exp, head_norm_w, w_ret_out, w_mlstm_out, w_o, w_up, ffn_conv_w, ffn_conv_b,
           w_down):
    batch, seq, d = x.shape
    ctx_len = ctx.shape[1]
    depth = w_ada.shape[0]
    vw = w_ret_out.shape[1]
    r_qk = (w_in.shape[2] - 4 * vw - 4 * H_M - 2 * d) // 4
    r_dk = r_qk // H_R
    m_dk = r_qk // H_M
    assert d == vw, "column-block addressing of the projection output assumes D_MODEL == VW"

    o_rv = 2 * r_qk
    o_mq = o_rv + 2 * vw
    o_mv = o_mq + 2 * r_qk
    o_mg = o_mv + 2 * vw
    o_merge = o_mg + 4 * H_M

    pad = (-(batch + 1)) % 8
    cvec = jnp.concatenate([c, c_ctx[None], jnp.zeros((pad, d), F32)], axis=0)
    mod = _ada(cvec, w_ada, b_ada)

    def mod_l(l, k):
        return mod[l, :batch, k * d:(k + 1) * d].reshape(batch, 1, d)

    def mod_c(l, k):
        return mod[l, batch:batch + 1, k * d:(k + 1) * d].reshape(1, 1, d)

    rope = _rope_tables(seq, r_dk, r_dk ** -0.5)
    qscale_r = jnp.concatenate([jnp.full((r_qk,), r_dk ** -0.5, F32), jnp.ones((r_qk,), F32)])
    log_gamma = jnp.log1p(-jnp.exp2(-ret_decay_exp.astype(F32)))

    ml, mc = batch * seq, batch * ctx_len
    xl = x.reshape(ml, d)
    xc = ctx.reshape(mc, d)
    hl = _normmod_call(xl, norm_w[0, 0], mod_l(0, 0), mod_l(0, 1), seq)
    hc = _normmod_call(xc, norm_w[0, 0], mod_c(0, 0), mod_c(0, 1), mc)

    w_in_b = w_in[:, :, :o_mg].astype(BF16)
    w_mg_b = w_in[:, :, o_merge:].astype(BF16)
    w_gate_t = jnp.transpose(w_in[:, :, o_mg:o_merge], (0, 2, 1)).astype(BF16)
    wr_b, wm_b, wo_b, wu_b, wd_b = (w.astype(BF16) for w in (w_ret_out, w_mlstm_out, w_o, w_up, w_down))
    d_ff = w_down.shape[1]

    for l in range(depth):
        last = l == depth - 1
        cw, cb = mlstm_conv_w[l], mlstm_conv_b[l]

        def project(h, seq_len, rope_tabs, nm):
            pa = _proj(h, w_in_b, l, ((o_rv, 2 * vw), (o_mv, 2 * vw)), name=nm + "_plain")
            pb = _proj(h, w_mg_b, l, ((0, 2 * d),), name=nm + "_mgate")
            if rope_tabs is None:
                rqk = _proj(h, w_in_b, l, ((0, o_rv),), colscale=qscale_r, name=nm + "_rqk")
            else:
                rqk = _proj(h, w_in_b, l, ((0, o_rv),), rope=rope_tabs, seq=seq_len, name=nm + "_rqk")
            mq, mk = _convproj(h, w_in_b, l, o_mq, r_qk, cw, cb, seq_len, scale_a=m_dk ** -0.5,
                               name=nm + "_mqk")
            b = h.shape[0] // seq_len
            return (pa, pb, rqk.reshape(b, seq_len, -1),
                    (mq.reshape(b, seq_len, -1), mk.reshape(b, seq_len, -1)),
                    _gates_t(h, w_gate_t, l, b, seq_len, nm + "_gates"))

        pa_c, pb_c, rqk_c, mqk_c, g_c = project(hc, ctx_len, None, "ctx")
        pa_l, pb_l, rqk_l, mqk_l, g_l = project(hl, seq, rope, "lat")
        pv_c = pa_c.reshape(batch, ctx_len, -1)
        pv_l = pa_l.reshape(batch, seq, -1)

        yr_c, yr_l = _retention(log_gamma[l], rqk_c, pv_c, rqk_l, pv_l, head_norm_w[l, 0],
                                0 * H_R, batch)
        ym_c, ym_l = _mlstm(mlstm_gate_b[l], *mqk_c, pv_c, g_c, *mqk_l, pv_l, g_l, head_norm_w[l, 1],
                            2 * H_M, batch)

        def tail(yr, ym, pa, pb, xres, mod_k, rows_per_mod, seq_len, nm):
            y = _merge(yr.reshape(-1, vw), ym.reshape(-1, vw), pa, pb, wr_b, wm_b, l)
            x1, h2 = _resid(y, wo_b, l, xres, norm_w[l, 1], mod_k(l, 2), rows_per_mod,
                            nxt=(norm_w[l, 2], mod_k(l, 3), mod_k(l, 4)), name=nm + "_o")
            act = _convproj(h2, wu_b, l, 0, d_ff, ffn_conv_w[l], ffn_conv_b[l], seq_len, name=nm + "_ffn")
            if last:
                return _resid(act, wd_b, l, x1, norm_w[l, 3], mod_k(l, 5), rows_per_mod,
                              name=nm + "_down"), None
            return _resid(act, wd_b, l, x1, norm_w[l, 3], mod_k(l, 5), rows_per_mod,
                          nxt=(norm_w[l + 1, 0], mod_k(l + 1, 0), mod_k(l + 1, 1)), name=nm + "_down")

        xl, hl = tail(yr_l, ym_l, pa_l, pb_l, xl, mod_l, seq, seq, "lat")
        if not last:
            xc, hc = tail(yr_c, ym_c, pa_c, pb_c, xc, mod_c, mc, ctx_len, "ctx")

    return xl.reshape(batch, seq, d)
```

```python
import functools

import jax
import jax.numpy as jnp
from jax import lax
from jax.experimental import pallas as pl
from jax.experimental.pallas import tpu as pltpu

F32 = jnp.float32
BF16 = jnp.bfloat16

GRID_W = 64
CHUNK = 128
H_R = 8
H_M = 8
N_MOD = 6
CONV_W = 3
ROPE_BASE = 10000.0
EPS = 1e-6
LOG2E = 1.4426950408889634
HALO = 16
ROWS_PART = 512
V7X_VMEM_LIMIT = 56 * 1024 * 1024

BM_PROJ = 1024
BN_PROJ = 2048
BM_CONV = 1024
BN_CONV = 512
BM_ROW = 256
BM_NORM = 512
BN_ADA = 1024
SCAN_UNROLL = 4


def _tile(n, pref, quantum):
    if n <= pref:
        return n
    t = (pref // quantum) * quantum
    while t > quantum and n % t:
        t -= quantum
    assert n % t == 0, (n, pref, quantum)
    return t


def _params(*sem):
    return pltpu.CompilerParams(dimension_semantics=sem, vmem_limit_bytes=V7X_VMEM_LIMIT)


def _sigmoid(x):
    return 1.0 / (1.0 + jnp.exp(-x))


def _silu(x):
    return x * _sigmoid(x)


def _log_sigmoid(x):
    return jnp.minimum(x, 0.0) - jnp.log1p(jnp.exp(-jnp.abs(x)))


def _rms(x, g):
    ms = jnp.mean(x * x, axis=-1, keepdims=True)
    return x * lax.rsqrt(ms + EPS) * g


def _normmod(x, g, shift, scale):
    return _rms(x, g) * (1.0 + scale) + shift


def _dot(a, b):
    return jnp.dot(a, b, preferred_element_type=F32)


def _dot_nt(a, b):
    return lax.dot_general(a, b, (((1,), (1,)), ((), ())), preferred_element_type=F32)


def _dot_tn(a, b):
    return lax.dot_general(a, b, (((0,), (0,)), ((), ())), preferred_element_type=F32)


def _ada_kernel(c_ref, w_ref, b_ref, o_ref):
    s = _silu(c_ref[...]).astype(BF16)
    o_ref[0] = _dot(s, w_ref[0].astype(BF16)) + b_ref[0]


def _ada(cvec, w_ada, b_ada):
    depth, d, n = w_ada.shape
    r = cvec.shape[0]
    bn = _tile(n, BN_ADA, 128)
    return pl.pallas_call(
        _ada_kernel,
        grid=(depth, n // bn),
        in_specs=[pl.BlockSpec((r, d), lambda l, j: (0, 0)),
                  pl.BlockSpec((1, d, bn), lambda l, j: (l, 0, j)),
                  pl.BlockSpec((1, 1, bn), lambda l, j: (l, 0, j))],
        out_specs=pl.BlockSpec((1, r, bn), lambda l, j: (l, 0, j)),
        out_shape=jax.ShapeDtypeStruct((depth, r, n), F32),
        compiler_params=_params("arbitrary", "arbitrary"),
        name="ada",
    )(cvec, w_ada, b_ada.reshape(depth, 1, n))


def _normmod_kernel(x_ref, g_ref, sh_ref, sc_ref, o_ref):
    o_ref[...] = _normmod(x_ref[...], g_ref[...], sh_ref[0], sc_ref[0]).astype(o_ref.dtype)


def _normmod_call(x, g, shift, scale, rows_per_mod):
    m, d = x.shape
    bm = _tile(rows_per_mod, BM_NORM, 16)
    per = rows_per_mod // bm
    vec = pl.BlockSpec((1, 1, d), lambda i: (i // per, 0, 0))
    return pl.pallas_call(
        _normmod_kernel,
        grid=(m // bm,),
        in_specs=[pl.BlockSpec((bm, d), lambda i: (i, 0)),
                  pl.BlockSpec((1, d), lambda i: (0, 0)), vec, vec],
        out_specs=pl.BlockSpec((bm, d), lambda i: (i, 0)),
        out_shape=jax.ShapeDtypeStruct((m, d), BF16),
        compiler_params=_params("parallel"),
        name="normmod",
    )(x, g.reshape(1, d), shift, scale)


def _proj_kernel(h_ref, w_ref, o_ref):
    o_ref[...] = _dot(h_ref[...], w_ref[...]).astype(o_ref.dtype)


def _proj_scale_kernel(h_ref, w_ref, s_ref, o_ref):
    o_ref[...] = (_dot(h_ref[...], w_ref[...]) * s_ref[...]).astype(o_ref.dtype)


def _swap32(r):
    lane = lax.broadcasted_iota(jnp.int32, r.shape, 1)
    return jnp.where((lane & 32) == 0, pltpu.roll(r, 96, axis=1), pltpu.roll(r, 32, axis=1))


def _row_parts(bm, rows):
    rb = rows if bm % rows == 0 else bm
    return [(r, rb) for r in range(0, bm, rb)]


def _proj_rope_kernel(h_ref, w_ref, c_ref, s_ref, o_ref):
    for r0, rb in _row_parts(h_ref.shape[0], ROWS_PART):
        rs = slice(r0, r0 + rb)
        res = _dot(h_ref[rs, :], w_ref[...])
        cos, sin = c_ref[0, rs, :], s_ref[0, rs, :]
        for g in range(res.shape[1] // 128):
            r = res[:, g * 128:(g + 1) * 128]
            o_ref[rs, g * 128:(g + 1) * 128] = (r * cos + _swap32(r) * sin).astype(o_ref.dtype)


def _col_block_map(ranges, bn):
    assert all(first % bn == 0 and cnt % bn == 0 for first, cnt in ranges), (ranges, bn)

    def block(j):
        blk = j + ranges[0][0] // bn
        seen = 0
        for (first, cnt), (nxt, _) in zip(ranges[:-1], ranges[1:]):
            seen += cnt // bn
            blk = blk + jnp.where(j >= seen, (nxt - first - cnt) // bn, 0)
        return blk
    return block


def _proj(h, w, layer, ranges, colscale=None, rope=None, seq=None, name="proj"):
    m, d = h.shape
    n = sum(cnt for _, cnt in ranges)
    bm = _tile(m if seq is None else seq, BM_PROJ, 16)
    bn = _tile(n, BN_PROJ, 128) if rope is None else n // 2
    block = _col_block_map(ranges, bn)
    in_specs = [pl.BlockSpec((bm, d), lambda i, j: (i, 0)),
                pl.BlockSpec((None, d, bn), lambda i, j: (layer, 0, block(j)))]
    args = [h, w]
    kern = _proj_kernel
    if colscale is not None:
        kern = _proj_scale_kernel
        in_specs.append(pl.BlockSpec((1, bn), lambda i, j: (0, j)))
        args.append(colscale.reshape(1, n))
    if rope is not None:
        kern = _proj_rope_kernel
        cos, sin = rope
        assert cos.shape == (n // bn, seq, 128)
        per = seq // bm
        tab = pl.BlockSpec((1, bm, 128), lambda i, j: (j, i % per, 0))
        in_specs += [tab, tab]
        args += [cos, sin]
    return pl.pallas_call(
        kern,
        grid=(m // bm, n // bn),
        in_specs=in_specs,
        out_specs=pl.BlockSpec((bm, bn), lambda i, j: (i, j)),
        out_shape=jax.ShapeDtypeStruct((m, n), BF16),
        compiler_params=_params("parallel", "arbitrary"),
        name=name,
    )(*args)


def _fill_halo(hp_ref, hc_ref, hn_ref, hext, first, last, bm):
    hext[0:HALO, :] = hp_ref[...]
    hext[HALO:HALO + bm, :] = hc_ref[...]
    hext[HALO + bm:, :] = hn_ref[...]

    @pl.when(first)
    def _():
        hext[0:HALO, :] = jnp.zeros((HALO, hext.shape[1]), hext.dtype)

    @pl.when(last)
    def _():
        hext[HALO + bm:, :] = jnp.zeros((HALO, hext.shape[1]), hext.dtype)


def _conv3(u, cw, cb, bm):
    rows = u.shape[0]
    prev = pltpu.roll(u, 1, axis=0)[HALO:HALO + bm]
    cur = u[HALO:HALO + bm]
    nxt = pltpu.roll(u, rows - 1, axis=0)[HALO:HALO + bm]
    return cb + prev * cw[0:1] + cur * cw[1:2] + nxt * cw[2:3]


def _convproj_pair_kernel(hp_ref, hc_ref, hn_ref, wa_ref, wg_ref, cwa_ref, cba_ref, cwg_ref, cbg_ref,
                          oa_ref, og_ref, hext, *, bm, per, scale_a):
    i, j = pl.program_id(0), pl.program_id(1)

    @pl.when(j == 0)
    def _():
        _fill_halo(hp_ref, hc_ref, hn_ref, hext, i % per == 0, i % per == per - 1, bm)

    hx = hext[...]
    a = _conv3(_dot(hx, wa_ref[...]), cwa_ref[...], cba_ref[...], bm)
    oa_ref[...] = (_silu(a) * scale_a).astype(oa_ref.dtype)
    g = _conv3(_dot(hx, wg_ref[...]), cwg_ref[...], cbg_ref[...], bm)
    og_ref[...] = _silu(g).astype(og_ref.dtype)


def _convproj_glu_kernel(hp_ref, hc_ref, hn_ref, wa_ref, wg_ref, cwa_ref, cba_ref, cwg_ref, cbg_ref,
                         o_ref, hext, *, bm, per):
    i, j = pl.program_id(0), pl.program_id(1)

    @pl.when(j == 0)
    def _():
        _fill_halo(hp_ref, hc_ref, hn_ref, hext, i % per == 0, i % per == per - 1, bm)

    hx = hext[...]
    a = _conv3(_dot(hx, wa_ref[...]), cwa_ref[...], cba_ref[...], bm)
    g = _conv3(_dot(hx, wg_ref[...]), cwg_ref[...], cbg_ref[...], bm)
    o_ref[...] = (_silu(a) * g).astype(o_ref.dtype)


def _convproj(h, w, layer, col0, n, cw, cb, seq, scale_a=None, name="convproj"):
    glu = scale_a is None
    m, d = h.shape
    bm = _tile(seq, BM_CONV, HALO)
    bn = _tile(n, BN_CONV, 128)
    assert col0 % bn == 0
    first = col0 // bn
    per = seq // bm
    nb = bm // HALO
    last_blk = m // HALO - 1
    hspecs = [pl.BlockSpec((HALO, d), lambda i, j: (jnp.maximum(i * nb - 1, 0), 0)),
              pl.BlockSpec((bm, d), lambda i, j: (i, 0)),
              pl.BlockSpec((HALO, d), lambda i, j: (jnp.minimum((i + 1) * nb, last_blk), 0))]
    wspec = lambda off: pl.BlockSpec((None, d, bn), lambda i, j: (layer, 0, first + j + off))
    cwspec = lambda off: pl.BlockSpec((CONV_W, bn), lambda i, j: (0, j + off))
    cbspec = lambda off: pl.BlockSpec((1, bn), lambda i, j: (0, j + off))
    cb2 = cb.reshape(1, -1)
    off = n // bn
    if glu:
        kern = functools.partial(_convproj_glu_kernel, bm=bm, per=per)
    else:
        kern = functools.partial(_convproj_pair_kernel, bm=bm, per=per, scale_a=scale_a)
    out_spec = pl.BlockSpec((bm, bn), lambda i, j: (i, j))
    out_shape = jax.ShapeDtypeStruct((m, n), BF16)
    return pl.pallas_call(
        kern,
        grid=(m // bm, n // bn),
        in_specs=hspecs + [wspec(0), wspec(off), cwspec(0), cbspec(0), cwspec(off), cbspec(off)],
        out_specs=out_spec if glu else [out_spec, out_spec],
        out_shape=out_shape if glu else [out_shape, out_shape],
        scratch_shapes=[pltpu.VMEM((bm + 2 * HALO, d), BF16)],
        compiler_params=_params("parallel", "arbitrary"),
        name=name,
    )(h, h, h, w, w, cw, cb2, cw, cb2)


def _chunk(c):
    return pl.ds(pl.multiple_of(c * CHUNK, CHUNK), CHUNK)


def _retention_kernel(lg_ref, qc_ref, kc_ref, vc_ref, ql_ref, kl_ref, vl_ref, g_ref,
                      oc_ref, ol_ref, st):
    h = pl.program_id(1)
    L = CHUNK
    dk = qc_ref.shape[-1]
    dv = vc_ref.shape[-1]
    assert dk == L
    nc = qc_ref.shape[1] // L
    nl = ql_ref.shape[1] // L
    lgf, lgb = lg_ref[0, h], lg_ref[1, h]
    row = lax.broadcasted_iota(jnp.int32, (L, L), 0).astype(F32)
    col = lax.broadcasted_iota(jnp.int32, (L, L), 1).astype(F32)
    rel = row - col
    mask = (jnp.where(rel >= 0, jnp.exp(jnp.maximum(rel, 0.0) * lgf), 0.0)
            + jnp.where(rel <= 0, jnp.exp(jnp.maximum(-rel, 0.0) * lgb), 0.0))
    qdec_f = jnp.exp((row + 1.0) * lgf)
    kdec_f = jnp.exp((L - 1.0 - row) * lgf)
    qdec_b = jnp.exp((L - row) * lgb)
    kdec_b = jnp.exp(row * lgb)
    cdec_f = jnp.exp(jnp.full((1, dv), L, F32) * lgf)
    cdec_b = jnp.exp(jnp.full((1, dv), L, F32) * lgb)

    def phase_a(k_ref, v_ref, n, base, carry):
        def body(t, carry):
            Sf, Sb = carry
            cf, cb = t, n - 1 - t
            st[base + cf, 0:dk, :] = Sf.astype(BF16)
            st[base + cb, dk:, :] = Sb.astype(BF16)
            kf = (k_ref[0, _chunk(cf), :].astype(F32) * kdec_f).astype(BF16)
            kb = (k_ref[0, _chunk(cb), :].astype(F32) * kdec_b).astype(BF16)
            Sf = cdec_f * Sf + _dot_tn(kf, v_ref[0, _chunk(cf), :])
            Sb = cdec_b * Sb + _dot_tn(kb, v_ref[0, _chunk(cb), :])
            return Sf, Sb
        return lax.fori_loop(0, n, body, carry, unroll=SCAN_UNROLL)

    def phase_b(q_ref, k_ref, v_ref, o_ref, n, base):
        g = g_ref[...]

        def body(c, carry):
            sl = _chunk(c)
            q, k, v = q_ref[0, sl, :], k_ref[0, sl, :], v_ref[0, sl, :]
            s = (_dot_nt(q, k) * mask).astype(BF16)
            qf = q.astype(F32)
            lhs = jnp.concatenate([s, (qf * qdec_f).astype(BF16), (qf * qdec_b).astype(BF16)], axis=1)
            y = _dot(lhs, jnp.concatenate([v, st[base + c]], axis=0))
            o_ref[0, sl, :] = _rms(y, g).astype(o_ref.dtype)
            return carry
        lax.fori_loop(0, n, body, 0, unroll=4 * SCAN_UNROLL)

    zero = jnp.zeros((dk, dv), F32)
    carry = phase_a(kc_ref, vc_ref, nc, 0, (zero, zero))
    phase_a(kl_ref, vl_ref, nl, nc, carry)
    phase_b(qc_ref, kc_ref, vc_ref, oc_ref, nc, 0)
    phase_b(ql_ref, kl_ref, vl_ref, ol_ref, nl, nc)


def _retention(log_gamma, qk_c, pv_c, qk_l, pv_l, g, v_blk, batch):
    tc, tl = qk_c.shape[1], qk_l.shape[1]
    dk = qk_c.shape[2] // (2 * H_R)
    dv = g.shape[0] // H_R
    qs = lambda t: pl.BlockSpec((1, t, dk), lambda b, h: (b, 0, h))
    ks = lambda t: pl.BlockSpec((1, t, dk), lambda b, h: (b, 0, H_R + h))
    vs = lambda t: pl.BlockSpec((1, t, dv), lambda b, h: (b, 0, v_blk + h))
    os_ = lambda t: pl.BlockSpec((1, t, dv), lambda b, h: (b, 0, h))
    return pl.pallas_call(
        _retention_kernel,
        grid=(batch, H_R),
        in_specs=[pl.BlockSpec(memory_space=pltpu.SMEM),
                  qs(tc), ks(tc), vs(tc), qs(tl), ks(tl), vs(tl),
                  pl.BlockSpec((1, dv), lambda b, h: (0, h))],
        out_specs=[os_(tc), os_(tl)],
        out_shape=[jax.ShapeDtypeStruct((batch, tc, H_R * dv), BF16),
                   jax.ShapeDtypeStruct((batch, tl, H_R * dv), BF16)],
        scratch_shapes=[pltpu.VMEM(((tc + tl) // CHUNK, 2 * dk, dv), BF16)],
        compiler_params=_params("parallel", "arbitrary"),
        name="retention",
    )(log_gamma, qk_c, qk_c, pv_c, qk_l, qk_l, pv_l, g.reshape(1, -1))


def _lane_scan(x, reverse, op, identity):
    n = x.shape[-1]
    lane = lax.broadcasted_iota(jnp.int32, x.shape, 1)
    s = 1
    while s < n:
        if reverse:
            x = op(x, jnp.where(lane < n - s, pltpu.roll(x, n - s, axis=1), identity))
        else:
            x = op(x, jnp.where(lane >= s, pltpu.roll(x, s, axis=1), identity))
        s *= 2
    return x


_B, _R, _CM, _W, _BE, _ML = range(6)


def _col_of_row(r):
    return jnp.broadcast_to(r, (r.shape[1], r.shape[1])).T


def _mlstm_kernel(gb_ref, qc_ref, kc_ref, vc_ref, gic_ref, gfc_ref, gibc_ref, gfbc_ref,
                  ql_ref, kl_ref, vl_ref, gil_ref, gfl_ref, gibl_ref, gfbl_ref, g_ref,
                  oc_ref, ol_ref, cst, mst, kt, sfc, sbc, sfl, sbl):
    h = pl.program_id(1)
    L = CHUNK
    dk = qc_ref.shape[-1]
    dv = vc_ref.shape[-1]
    assert dk == L
    nc = qc_ref.shape[1] // L
    nl = ql_ref.shape[1] // L
    row = lax.broadcasted_iota(jnp.int32, (L, L), 0)
    col = lax.broadcasted_iota(jnp.int32, (L, L), 1)
    tri = (col <= row, col >= row)
    end_lane = (L - 1, 0)

    ones_col = jnp.ones((L, L), BF16)

    def v_aug(v_ref, c):
        return jnp.concatenate([v_ref[0, _chunk(c), :], ones_col], axis=1)

    def prep(gi_ref, gf_ref, scr, d):
        rev = d == 1
        ig = (gi_ref[0, 0] + gb_ref[2 * d, h]) * LOG2E
        lf = _log_sigmoid(gf_ref[0, 0] + gb_ref[2 * d + 1, h]) * LOG2E
        b = _lane_scan(lf, rev, jnp.add, 0.0)
        r = ig - b
        lane = lax.broadcasted_iota(jnp.int32, b.shape, 1)
        be = jnp.broadcast_to(
            jnp.sum(jnp.where(lane == end_lane[d], b, 0.0), axis=-1, keepdims=True), b.shape)
        loc = be + r
        m_loc = jnp.broadcast_to(jnp.max(loc, axis=-1, keepdims=True), b.shape)
        scr[_B] = b
        scr[_R] = r
        scr[_CM] = _lane_scan(r, rev, jnp.maximum, -jnp.inf)
        scr[_W] = jnp.exp2(loc - m_loc)
        scr[_BE] = be
        scr[_ML] = m_loc

    prep(gic_ref, gfc_ref, sfc, 0)
    prep(gibc_ref, gfbc_ref, sbc, 1)
    prep(gil_ref, gfl_ref, sfl, 0)
    prep(gibl_ref, gfbl_ref, sbl, 1)

    def transpose_k(k_ref, n, base):
        def body(c, carry):
            kt[base + c] = k_ref[0, _chunk(c), :].T
            return carry
        lax.fori_loop(0, n, body, 0, unroll=SCAN_UNROLL)

    transpose_k(kc_ref, nc, 0)
    transpose_k(kl_ref, nl, nc)

    def phase_a(v_ref, scr, n, base, carry):
        def step(d, c, state):
            C, m = state
            s = scr[d]
            cst[d, base + c] = C.astype(BF16)
            mst[d, base + c] = m
            row_of = lambda idx: s[idx, pl.ds(c, 1), :]
            kw = (kt[base + c].astype(F32) * row_of(_W)).astype(BF16)
            dC = _dot(kw, v_aug(v_ref, c))
            be, m_loc = row_of(_BE), row_of(_ML)
            m_new = jnp.maximum(be + m, m_loc)
            f_old = jnp.exp2(be + m - m_new)
            f_new = jnp.exp2(m_loc - m_new)
            wide = lambda f: jnp.concatenate([f] * (C.shape[1] // L), axis=1)
            return wide(f_old) * C + wide(f_new) * dC, m_new

        def body(t, carry):
            return step(0, t, carry[0]), step(1, n - 1 - t, carry[1])
        return lax.fori_loop(0, n, body, carry, unroll=SCAN_UNROLL)

    def phase_b(q_ref, k_ref, v_ref, o_ref, scr, n, base):
        g = g_ref[...]
        ones_dv = jnp.ones((dv, L), BF16)

        def body(c, carry):
            sl = _chunk(c)
            q, k = q_ref[0, sl, :], k_ref[0, sl, :]
            va = v_aug(v_ref, c)
            qk = _dot_nt(q, k)
            qf = q.astype(F32)
            y = None
            for d in (0, 1):
                s = scr[d]
                row_of = lambda idx: s[idx, pl.ds(c, 1), :]
                m_prev = mst[d, base + c]
                m_t = row_of(_B) + jnp.maximum(row_of(_CM), m_prev)
                U = _col_of_row(row_of(_B) - m_t)
                NM = _col_of_row(-m_t)
                p = jnp.exp2(jnp.where(tri[d], U + row_of(_R), -jnp.inf))
                a = jnp.exp2(U + m_prev)
                lhs = jnp.concatenate([(qk * p).astype(BF16), (a * qf).astype(BF16)], axis=1)
                rhs = jnp.concatenate([va, cst[d, base + c]], axis=0)
                res = _dot(lhs, rhs)
                inv = 1.0 / jnp.maximum(jnp.abs(res[:, dv:]), jnp.exp2(NM))
                hval = res[:, :dv] * jnp.concatenate([inv] * (dv // L), axis=1)
                y = hval if y is None else y + hval
            ms = _dot((y * y).astype(BF16), ones_dv) * (1.0 / dv)
            scale = lax.rsqrt(ms + EPS)
            o_ref[0, sl, :] = (y * jnp.concatenate([scale] * (dv // L), axis=1) * g).astype(o_ref.dtype)
            return carry
        lax.fori_loop(0, n, body, 0, unroll=8 * SCAN_UNROLL)

    zero = (jnp.zeros((dk, dv + L), F32), jnp.zeros((1, L), F32))
    scr_c = (sfc, sbc)
    scr_l = (sfl, sbl)
    carry = phase_a(vc_ref, scr_c, nc, 0, (zero, zero))
    phase_a(vl_ref, scr_l, nl, nc, carry)
    phase_b(qc_ref, kc_ref, vc_ref, oc_ref, scr_c, nc, 0)
    phase_b(ql_ref, kl_ref, vl_ref, ol_ref, scr_l, nl, nc)


def _mlstm(gate_b, q_c, k_c, pv_c, gates_c, q_l, k_l, pv_l, gates_l, g, v_blk, batch):
    tc, tl = q_c.shape[1], q_l.shape[1]
    dk = q_c.shape[2] // H_M
    dv = g.shape[0] // H_M
    qs = lambda t: pl.BlockSpec((1, t, dk), lambda b, h: (b, 0, h))
    ks = qs
    vs = lambda t: pl.BlockSpec((1, t, dv), lambda b, h: (b, 0, v_blk + h))
    gs = lambda t, k: pl.BlockSpec((1, 1, t // CHUNK, CHUNK), lambda b, h: (b, k * H_M + h, 0, 0))
    os_ = lambda t: pl.BlockSpec((1, t, dv), lambda b, h: (b, 0, h))

    def stream(t):
        return [qs(t), ks(t), vs(t), gs(t, 0), gs(t, 1), gs(t, 2), gs(t, 3)]

    n_slots = (tc + tl) // CHUNK
    rows = lambda t: pltpu.VMEM((6, t // CHUNK, CHUNK), F32)
    return pl.pallas_call(
        _mlstm_kernel,
        grid=(batch, H_M),
        in_specs=[pl.BlockSpec(memory_space=pltpu.SMEM)] + stream(tc) + stream(tl)
                 + [pl.BlockSpec((1, dv), lambda b, h: (0, h))],
        out_specs=[os_(tc), os_(tl)],
        out_shape=[jax.ShapeDtypeStruct((batch, tc, H_M * dv), BF16),
                   jax.ShapeDtypeStruct((batch, tl, H_M * dv), BF16)],
        scratch_shapes=[pltpu.VMEM((2, n_slots, dk, dv + CHUNK), BF16),
                        pltpu.VMEM((2, n_slots, 1, CHUNK), F32),
                        pltpu.VMEM((n_slots, dk, CHUNK), BF16),
                        rows(tc), rows(tc), rows(tl), rows(tl)],
        compiler_params=_params("parallel", "arbitrary"),
        name="mlstm",
    )(gate_b, q_c, k_c, pv_c, gates_c, gates_c, gates_c, gates_c,
      q_l, k_l, pv_l, gates_l, gates_l, gates_l, gates_l, g.reshape(1, -1))


def _merge_kernel(yr_ref, rg_ref, ym_ref, mo_ref, gr_ref, gm_ref, wr_ref, wm_ref, o_ref):
    a = (yr_ref[...] * _silu(rg_ref[...].astype(F32)).astype(BF16))
    b = (ym_ref[...] * _sigmoid(mo_ref[...].astype(F32)).astype(BF16))
    y = (_sigmoid(gr_ref[...].astype(F32)) * _dot(a, wr_ref[...])
         + _sigmoid(gm_ref[...].astype(F32)) * _dot(b, wm_ref[...]))
    o_ref[...] = y.astype(o_ref.dtype)


def _merge(yr, ym, pa, pb, wr, wm, layer):
    m, vw = yr.shape
    d = wr.shape[2]
    bm = _tile(m, BM_ROW, 16)
    col = lambda k: pl.BlockSpec((bm, vw), lambda i: (i, k))
    wspec = pl.BlockSpec((None, vw, d), lambda i: (layer, 0, 0), pipeline_mode=pl.Buffered(1))
    return pl.pallas_call(
        _merge_kernel,
        grid=(m // bm,),
        in_specs=[col(0), col(1), col(0), col(3), col(0), col(1), wspec, wspec],
        out_specs=pl.BlockSpec((bm, d), lambda i: (i, 0)),
        out_shape=jax.ShapeDtypeStruct((m, d), BF16),
        compiler_params=_params("parallel"),
        name="merge",
    )(yr, pa, ym, pa, pb, pb, wr, wm)


def _resid_kernel(a_ref, w_ref, x_ref, go_ref, gate_ref, gn_ref, sh_ref, sc_ref, xo_ref, ho_ref):
    for r, rb in _row_parts(a_ref.shape[0], BM_ROW):
        rs = slice(r, r + rb)
        out = _dot(a_ref[rs, :], w_ref[...])
        xn = x_ref[rs, :] + gate_ref[0] * _rms(out, go_ref[...])
        xo_ref[rs, :] = xn
        ho_ref[rs, :] = _normmod(xn, gn_ref[...], sh_ref[0], sc_ref[0]).astype(ho_ref.dtype)


def _resid_last_kernel(a_ref, w_ref, x_ref, go_ref, gate_ref, xo_ref):
    for r, rb in _row_parts(a_ref.shape[0], BM_ROW):
        rs = slice(r, r + rb)
        out = _dot(a_ref[rs, :], w_ref[...])
        xo_ref[rs, :] = x_ref[rs, :] + gate_ref[0] * _rms(out, go_ref[...])


def _resid(a, w, layer, x, g_out, gate, rows_per_mod, nxt=None, name="resid"):
    m, k = a.shape
    d = w.shape[2]
    bm = _tile(rows_per_mod, 2 * BM_ROW, 16)
    tiles = 2 * bm * (2 * k + 4 * d + 4 * d + 2 * d) + bm * 4 * d
    if 2 * k * d + tiles > V7X_VMEM_LIMIT:
        bm = _tile(rows_per_mod, BM_ROW, 16)
    per = rows_per_mod // bm
    row = lambda width: pl.BlockSpec((bm, width), lambda i: (i, 0))
    vec = pl.BlockSpec((1, 1, d), lambda i: (i // per, 0, 0))
    gvec = pl.BlockSpec((1, d), lambda i: (0, 0))
    in_specs = [row(k),
                pl.BlockSpec((None, k, d), lambda i: (layer, 0, 0), pipeline_mode=pl.Buffered(1)),
                row(d), gvec, vec]
    args = [a, w, x, g_out.reshape(1, d), gate]
    if nxt is None:
        return pl.pallas_call(
            _resid_last_kernel, grid=(m // bm,), in_specs=in_specs, out_specs=row(d),
            out_shape=jax.ShapeDtypeStruct((m, d), F32),
            compiler_params=_params("parallel"), name=name,
        )(*args)
    g_next, shift, scale = nxt
    return pl.pallas_call(
        _resid_kernel, grid=(m // bm,),
        in_specs=in_specs + [gvec, vec, vec],
        out_specs=[row(d), row(d)],
        out_shape=[jax.ShapeDtypeStruct((m, d), F32), jax.ShapeDtypeStruct((m, d), BF16)],
        compiler_params=_params("parallel"), name=name,
    )(*args, g_next.reshape(1, d), shift, scale)


def _rope_tables(seq, dk, q_scale):
    quarter = dk // 4
    t = jnp.arange(seq)
    rows = (t // GRID_W).astype(F32)
    cols = (t % GRID_W).astype(F32)
    freqs = ROPE_BASE ** (-jnp.arange(quarter, dtype=F32) / quarter)
    ar = rows[:, None] * freqs[None, :]
    ac = cols[:, None] * freqs[None, :]
    cos = jnp.concatenate([jnp.cos(ar), jnp.cos(ar), jnp.cos(ac), jnp.cos(ac)], axis=-1)
    sin = jnp.concatenate([-jnp.sin(ar), jnp.sin(ar), -jnp.sin(ac), jnp.sin(ac)], axis=-1)
    scale = jnp.array([q_scale, 1.0], F32)[:, None, None]
    return cos[None] * scale, sin[None] * scale


def _gates_t_kernel(wt_ref, h_ref, o_ref):
    o_ref[0] = _dot_nt(wt_ref[...], h_ref[...])


def _gates_t(h, wt, layer, batch, seq, name):
    m, d = h.shape
    ng = wt.shape[1]
    bm = _tile(seq, BM_PROJ, 128)
    per = seq // bm
    out = pl.pallas_call(
        _gates_t_kernel,
        grid=(m // bm,),
        in_specs=[pl.BlockSpec((None, ng, d), lambda i: (layer, 0, 0)),
                  pl.BlockSpec((bm, d), lambda i: (i, 0))],
        out_specs=pl.BlockSpec((1, ng, bm), lambda i: (i // per, 0, i % per)),
        out_shape=jax.ShapeDtypeStruct((batch, ng, seq), F32),
        compiler_params=_params("parallel"),
        name=name,
    )(wt, h)
    return out.reshape(batch, ng, seq // CHUNK, CHUNK)


def kernel(x, c, ctx, c_ctx, w_ada, b_ada, norm_w, w_in, mlstm_conv_w, mlstm_conv_b, mlstm_gate_b,
           ret_decay_exp, head_norm_w, w_ret_out, w_mlstm_out, w_o, w_up, ffn_conv_w, ffn_conv_b,
           w_down):
    batch, seq, d = x.shape
    ctx_len = ctx.shape[1]
    depth = w_ada.shape[0]
    vw = w_ret_out.shape[1]
    r_qk = (w_in.shape[2] - 4 * vw - 4 * H_M - 2 * d) // 4
    r_dk = r_qk // H_R
    m_dk = r_qk // H_M
    assert d == vw, "column-block addressing of the projection output assumes D_MODEL == VW"

    o_rv = 2 * r_qk
    o_mq = o_rv + 2 * vw
    o_mv = o_mq + 2 * r_qk
    o_mg = o_mv + 2 * vw
    o_merge = o_mg + 4 * H_M

    pad = (-(batch + 1)) % 8
    cvec = jnp.concatenate([c, c_ctx[None], jnp.zeros((pad, d), F32)], axis=0)
    mod = _ada(cvec, w_ada, b_ada)

    def mod_l(l, k):
        return mod[l, :batch, k * d:(k + 1) * d].reshape(batch, 1, d)

    def mod_c(l, k):
        return mod[l, batch:batch + 1, k * d:(k + 1) * d].reshape(1, 1, d)

    rope = _rope_tables(seq, r_dk, r_dk ** -0.5)
    qscale_r = jnp.concatenate([jnp.full((r_qk,), r_dk ** -0.5, F32), jnp.ones((r_qk,), F32)])
    log_gamma = jnp.log1p(-jnp.exp2(-ret_decay_exp.astype(F32)))

    ml, mc = batch * seq, batch * ctx_len
    xl = x.reshape(ml, d)
    xc = ctx.reshape(mc, d)
    hl = _normmod_call(xl, norm_w[0, 0], mod_l(0, 0), mod_l(0, 1), seq)
    hc = _normmod_call(xc, norm_w[0, 0], mod_c(0, 0), mod_c(0, 1), mc)

    w_in_b = w_in[:, :, :o_mg].astype(BF16)
    w_mg_b = w_in[:, :, o_merge:].astype(BF16)
    w_gate_t = jnp.transpose(w_in[:, :, o_mg:o_merge], (0, 2, 1)).astype(BF16)
    wr_b, wm_b, wo_b, wu_b, wd_b = (w.astype(BF16) for w in (w_ret_out, w_mlstm_out, w_o, w_up, w_down))
    d_ff = w_down.shape[1]

    for l in range(depth):
        last = l == depth - 1
        cw, cb = mlstm_conv_w[l], mlstm_conv_b[l]

        def project(h, seq_len, rope_tabs, nm):
            pa = _proj(h, w_in_b, l, ((o_rv, 2 * vw), (o_mv, 2 * vw)), name=nm + "_plain")
            pb = _proj(h, w_mg_b, l, ((0, 2 * d),), name=nm + "_mgate")
            if rope_tabs is None:
                rqk = _proj(h, w_in_b, l, ((0, o_rv),), colscale=qscale_r, name=nm + "_rqk")
            else:
                rqk = _proj(h, w_in_b, l, ((0, o_rv),), rope=rope_tabs, seq=seq_len, name=nm + "_rqk")
            mq, mk = _convproj(h, w_in_b, l, o_mq, r_qk, cw, cb, seq_len, scale_a=m_dk ** -0.5,
                               name=nm + "_mqk")
            b = h.shape[0] // seq_len
            return (pa, pb, rqk.reshape(b, seq_len, -1),
                    (mq.reshape(b, seq_len, -1), mk.reshape(b, seq_len, -1)),
                    _gates_t(h, w_gate_t, l, b, seq_len, nm + "_gates"))

        pa_c, pb_c, rqk_c, mqk_c, g_c = project(hc, ctx_len, None, "ctx")
        pa_l, pb_l, rqk_l, mqk_l, g_l = project(hl, seq, rope, "lat")
        pv_c = pa_c.reshape(batch, ctx_len, -1)
        pv_l = pa_l.reshape(batch, seq, -1)

        yr_c, yr_l = _retention(log_gamma[l], rqk_c, pv_c, rqk_l, pv_l, head_norm_w[l, 0],
                                0 * H_R, batch)
        ym_c, ym_l = _mlstm(mlstm_gate_b[l], *mqk_c, pv_c, g_c, *mqk_l, pv_l, g_l, head_norm_w[l, 1],
                            2 * H_M, batch)

        def tail(yr, ym, pa, pb, xres, mod_k, rows_per_mod, seq_len, nm):
            y = _merge(yr.reshape(-1, vw), ym.reshape(-1, vw), pa, pb, wr_b, wm_b, l)
            x1, h2 = _resid(y, wo_b, l, xres, norm_w[l, 1], mod_k(l, 2), rows_per_mod,
                            nxt=(norm_w[l, 2], mod_k(l, 3), mod_k(l, 4)), name=nm + "_o")
            act = _convproj(h2, wu_b, l, 0, d_ff, ffn_conv_w[l], ffn_conv_b[l], seq_len, name=nm + "_ffn")
            if last:
                return _resid(act, wd_b, l, x1, norm_w[l, 3], mod_k(l, 5), rows_per_mod,
                              name=nm + "_down"), None
            return _resid(act, wd_b, l, x1, norm_w[l, 3], mod_k(l, 5), rows_per_mod,
                          nxt=(norm_w[l + 1, 0], mod_k(l + 1, 0), mod_k(l + 1, 1)), name=nm + "_down")

        xl, hl = tail(yr_l, ym_l, pa_l, pb_l, xl, mod_l, seq, seq, "lat")
        if not last:
            xc, hc = tail(yr_c, ym_c, pa_c, pb_c, xc, mod_c, mc, ctx_len, "ctx")

    return xl.reshape(batch, seq, d)
```

```python
import functools

import jax
import jax.numpy as jnp
from jax import lax
from jax.experimental import pallas as pl
from jax.experimental.pallas import tpu as pltpu

F32 = jnp.float32
BF16 = jnp.bfloat16

GRID_W = 64
CHUNK = 128
H_R = 8
H_M = 8
N_MOD = 6
CONV_W = 3
ROPE_BASE = 10000.0
EPS = 1e-6
LOG2E = 1.4426950408889634
HALO = 16
ROWS_PART = 512
V7X_VMEM_LIMIT = 56 * 1024 * 1024

BM_PROJ = 1024
BN_PROJ = 2048
BM_CONV = 1024
BN_CONV = 512
BM_ROW = 256
BM_NORM = 512
BN_ADA = 1024
SCAN_UNROLL = 4


def _tile(n, pref, quantum):
    if n <= pref:
        return n
    t = (pref // quantum) * quantum
    while t > quantum and n % t:
        t -= quantum
    assert n % t == 0, (n, pref, quantum)
    return t


def _params(*sem):
    return pltpu.CompilerParams(dimension_semantics=sem, vmem_limit_bytes=V7X_VMEM_LIMIT)


def _sigmoid(x):
    return 1.0 / (1.0 + jnp.exp(-x))


def _silu(x):
    return x * _sigmoid(x)


def _log_sigmoid(x):
    return jnp.minimum(x, 0.0) - jnp.log1p(jnp.exp(-jnp.abs(x)))


def _rms(x, g):
    ms = jnp.mean(x * x, axis=-1, keepdims=True)
    return x * lax.rsqrt(ms + EPS) * g


def _normmod(x, g, shift, scale):
    return _rms(x, g) * (1.0 + scale) + shift


def _dot(a, b):
    return jnp.dot(a, b, preferred_element_type=F32)


def _dot_nt(a, b):
    return lax.dot_general(a, b, (((1,), (1,)), ((), ())), preferred_element_type=F32)


def _dot_tn(a, b):
    return lax.dot_general(a, b, (((0,), (0,)), ((), ())), preferred_element_type=F32)


def _ada_kernel(c_ref, w_ref, b_ref, o_ref):
    s = _silu(c_ref[...]).astype(BF16)
    o_ref[0] = _dot(s, w_ref[0].astype(BF16)) + b_ref[0]


def _ada(cvec, w_ada, b_ada):
    depth, d, n = w_ada.shape
    r = cvec.shape[0]
    bn = _tile(n, BN_ADA, 128)
    return pl.pallas_call(
        _ada_kernel,
        grid=(depth, n // bn),
        in_specs=[pl.BlockSpec((r, d), lambda l, j: (0, 0)),
                  pl.BlockSpec((1, d, bn), lambda l, j: (l, 0, j)),
                  pl.BlockSpec((1, 1, bn), lambda l, j: (l, 0, j))],
        out_specs=pl.BlockSpec((1, r, bn), lambda l, j: (l, 0, j)),
        out_shape=jax.ShapeDtypeStruct((depth, r, n), F32),
        compiler_params=_params("arbitrary", "arbitrary"),
        name="ada",
    )(cvec, w_ada, b_ada.reshape(depth, 1, n))


def _normmod_kernel(x_ref, g_ref, sh_ref, sc_ref, o_ref):
    o_ref[...] = _normmod(x_ref[...], g_ref[...], sh_ref[0], sc_ref[0]).astype(o_ref.dtype)


def _normmod_call(x, g, shift, scale, rows_per_mod):
    m, d = x.shape
    bm = _tile(rows_per_mod, BM_NORM, 16)
    per = rows_per_mod // bm
    vec = pl.BlockSpec((1, 1, d), lambda i: (i // per, 0, 0))
    return pl.pallas_call(
        _normmod_kernel,
        grid=(m // bm,),
        in_specs=[pl.BlockSpec((bm, d), lambda i: (i, 0)),
                  pl.BlockSpec((1, d), lambda i: (0, 0)), vec, vec],
        out_specs=pl.BlockSpec((bm, d), lambda i: (i, 0)),
        out_shape=jax.ShapeDtypeStruct((m, d), BF16),
        compiler_params=_params("parallel"),
        name="normmod",
    )(x, g.reshape(1, d), shift, scale)


def _proj_kernel(h_ref, w_ref, o_ref):
    o_ref[...] = _dot(h_ref[...], w_ref[...]).astype(o_ref.dtype)


def _proj_scale_kernel(h_ref, w_ref, s_ref, o_ref):
    o_ref[...] = (_dot(h_ref[...], w_ref[...]) * s_ref[...]).astype(o_ref.dtype)


def _swap32(r):
    lane = lax.broadcasted_iota(jnp.int32, r.shape, 1)
    return jnp.where((lane & 32) == 0, pltpu.roll(r, 96, axis=1), pltpu.roll(r, 32, axis=1))


def _row_parts(bm, rows):
    rb = rows if bm % rows == 0 else bm
    return [(r, rb) for r in range(0, bm, rb)]


def _proj_rope_kernel(h_ref, w_ref, c_ref, s_ref, o_ref):
    for r0, rb in _row_parts(h_ref.shape[0], ROWS_PART):
        rs = slice(r0, r0 + rb)
        res = _dot(h_ref[rs, :], w_ref[...])
        cos, sin = c_ref[0, rs, :], s_ref[0, rs, :]
        for g in range(res.shape[1] // 128):
            r = res[:, g * 128:(g + 1) * 128]
            o_ref[rs, g * 128:(g + 1) * 128] = (r * cos + _swap32(r) * sin).astype(o_ref.dtype)


def _col_block_map(ranges, bn):
    assert all(first % bn == 0 and cnt % bn == 0 for first, cnt in ranges), (ranges, bn)

    def block(j):
        blk = j + ranges[0][0] // bn
        seen = 0
        for (first, cnt), (nxt, _) in zip(ranges[:-1], ranges[1:]):
            seen += cnt // bn
            blk = blk + jnp.where(j >= seen, (nxt - first - cnt) // bn, 0)
        return blk
    return block


def _proj(h, w, layer, ranges, colscale=None, rope=None, seq=None, name="proj"):
    m, d = h.shape
    n = sum(cnt for _, cnt in ranges)
    bm = _tile(m if seq is None else seq, BM_PROJ, 16)
    bn = _tile(n, BN_PROJ, 128) if rope is None else n // 2
    block = _col_block_map(ranges, bn)
    in_specs = [pl.BlockSpec((bm, d), lambda i, j: (i, 0)),
                pl.BlockSpec((None, d, bn), lambda i, j: (layer, 0, block(j)))]
    args = [h, w]
    kern = _proj_kernel
    if colscale is not None:
        kern = _proj_scale_kernel
        in_specs.append(pl.BlockSpec((1, bn), lambda i, j: (0, j)))
        args.append(colscale.reshape(1, n))
    if rope is not None:
        kern = _proj_rope_kernel
        cos, sin = rope
        assert cos.shape == (n // bn, seq, 128)
        per = seq // bm
        tab = pl.BlockSpec((1, bm, 128), lambda i, j: (j, i % per, 0))
        in_specs += [tab, tab]
        args += [cos, sin]
    return pl.pallas_call(
        kern,
        grid=(m // bm, n // bn),
        in_specs=in_specs,
        out_specs=pl.BlockSpec((bm, bn), lambda i, j: (i, j)),
        out_shape=jax.ShapeDtypeStruct((m, n), BF16),
        compiler_params=_params("parallel", "arbitrary"),
        name=name,
    )(*args)


def _fill_halo(hp_ref, hc_ref, hn_ref, hext, first, last, bm):
    hext[0:HALO, :] = hp_ref[...]
    hext[HALO:HALO + bm, :] = hc_ref[...]
    hext[HALO + bm:, :] = hn_ref[...]

    @pl.when(first)
    def _():
        hext[0:HALO, :] = jnp.zeros((HALO, hext.shape[1]), hext.dtype)

    @pl.when(last)
    def _():
        hext[HALO + bm:, :] = jnp.zeros((HALO, hext.shape[1]), hext.dtype)


def _conv3(u, cw, cb, bm):
    rows = u.shape[0]
    prev = pltpu.roll(u, 1, axis=0)[HALO:HALO + bm]
    cur = u[HALO:HALO + bm]
    nxt = pltpu.roll(u, rows - 1, axis=0)[HALO:HALO + bm]
    return cb + prev * cw[0:1] + cur * cw[1:2] + nxt * cw[2:3]


def _convproj_pair_kernel(hp_ref, hc_ref, hn_ref, wa_ref, wg_ref, cwa_ref, cba_ref, cwg_ref, cbg_ref,
                          oa_ref, og_ref, hext, *, bm, per, scale_a):
    i, j = pl.program_id(0), pl.program_id(1)

    @pl.when(j == 0)
    def _():
        _fill_halo(hp_ref, hc_ref, hn_ref, hext, i % per == 0, i % per == per - 1, bm)

    hx = hext[...]
    a = _conv3(_dot(hx, wa_ref[...]), cwa_ref[...], cba_ref[...], bm)
    oa_ref[...] = (_silu(a) * scale_a).astype(oa_ref.dtype)
    g = _conv3(_dot(hx, wg_ref[...]), cwg_ref[...], cbg_ref[...], bm)
    og_ref[...] = _silu(g).astype(og_ref.dtype)


def _convproj_glu_kernel(hp_ref, hc_ref, hn_ref, wa_ref, wg_ref, cwa_ref, cba_ref, cwg_ref, cbg_ref,
                         o_ref, hext, *, bm, per):
    i, j = pl.program_id(0), pl.program_id(1)

    @pl.when(j == 0)
    def _():
        _fill_halo(hp_ref, hc_ref, hn_ref, hext, i % per == 0, i % per == per - 1, bm)

    hx = hext[...]
    a = _conv3(_dot(hx, wa_ref[...]), cwa_ref[...], cba_ref[...], bm)
    g = _conv3(_dot(hx, wg_ref[...]), cwg_ref[...], cbg_ref[...], bm)
    o_ref[...] = (_silu(a) * g).astype(o_ref.dtype)


def _convproj(h, w, layer, col0, n, cw, cb, seq, scale_a=None, name="convproj"):
    glu = scale_a is None
    m, d = h.shape
    bm = _tile(seq, BM_CONV, HALO)
    bn = _tile(n, BN_CONV, 128)
    assert col0 % bn == 0
    first = col0 // bn
    per = seq // bm
    nb = bm // HALO
    last_blk = m // HALO - 1
    hspecs = [pl.BlockSpec((HALO, d), lambda i, j: (jnp.maximum(i * nb - 1, 0), 0)),
              pl.BlockSpec((bm, d), lambda i, j: (i, 0)),
              pl.BlockSpec((HALO, d), lambda i, j: (jnp.minimum((i + 1) * nb, last_blk), 0))]
    wspec = lambda off: pl.BlockSpec((None, d, bn), lambda i, j: (layer, 0, first + j + off))
    cwspec = lambda off: pl.BlockSpec((CONV_W, bn), lambda i, j: (0, j + off))
    cbspec = lambda off: pl.BlockSpec((1, bn), lambda i, j: (0, j + off))
    cb2 = cb.reshape(1, -1)
    off = n // bn
    if glu:
        kern = functools.partial(_convproj_glu_kernel, bm=bm, per=per)
    else:
        kern = functools.partial(_convproj_pair_kernel, bm=bm, per=per, scale_a=scale_a)
    out_spec = pl.BlockSpec((bm, bn), lambda i, j: (i, j))
    out_shape = jax.ShapeDtypeStruct((m, n), BF16)
    return pl.pallas_call(
        kern,
        grid=(m // bm, n // bn),
        in_specs=hspecs + [wspec(0), wspec(off), cwspec(0), cbspec(0), cwspec(off), cbspec(off)],
        out_specs=out_spec if glu else [out_spec, out_spec],
        out_shape=out_shape if glu else [out_shape, out_shape],
        scratch_shapes=[pltpu.VMEM((bm + 2 * HALO, d), BF16)],
        compiler_params=_params("parallel", "arbitrary"),
        name=name,
    )(h, h, h, w, w, cw, cb2, cw, cb2)


def _chunk(c):
    return pl.ds(pl.multiple_of(c * CHUNK, CHUNK), CHUNK)


def _retention_kernel(lg_ref, qc_ref, kc_ref, vc_ref, ql_ref, kl_ref, vl_ref, g_ref,
                      oc_ref, ol_ref, st):
    h = pl.program_id(1)
    L = CHUNK
    dk = qc_ref.shape[-1]
    dv = vc_ref.shape[-1]
    assert dk == L
    nc = qc_ref.shape[1] // L
    nl = ql_ref.shape[1] // L
    lgf, lgb = lg_ref[0, h], lg_ref[1, h]
    row = lax.broadcasted_iota(jnp.int32, (L, L), 0).astype(F32)
    col = lax.broadcasted_iota(jnp.int32, (L, L), 1).astype(F32)
    rel = row - col
    mask = (jnp.where(rel >= 0, jnp.exp(jnp.maximum(rel, 0.0) * lgf), 0.0)
            + jnp.where(rel <= 0, jnp.exp(jnp.maximum(-rel, 0.0) * lgb), 0.0))
    qdec_f = jnp.exp((row + 1.0) * lgf)
    kdec_f = jnp.exp((L - 1.0 - row) * lgf)
    qdec_b = jnp.exp((L - row) * lgb)
    kdec_b = jnp.exp(row * lgb)
    cdec_f = jnp.exp(jnp.full((1, dv), L, F32) * lgf)
    cdec_b = jnp.exp(jnp.full((1, dv), L, F32) * lgb)

    def phase_a(k_ref, v_ref, n, base, carry):
        def body(t, carry):
            Sf, Sb = carry
            cf, cb = t, n - 1 - t
            st[base + cf, 0:dk, :] = Sf.astype(BF16)
            st[base + cb, dk:, :] = Sb.astype(BF16)
            kf = (k_ref[0, _chunk(cf), :].astype(F32) * kdec_f).astype(BF16)
            kb = (k_ref[0, _chunk(cb), :].astype(F32) * kdec_b).astype(BF16)
            Sf = cdec_f * Sf + _dot_tn(kf, v_ref[0, _chunk(cf), :])
            Sb = cdec_b * Sb + _dot_tn(kb, v_ref[0, _chunk(cb), :])
            return Sf, Sb
        return lax.fori_loop(0, n, body, carry, unroll=2 * SCAN_UNROLL)

    def phase_b(q_ref, k_ref, v_ref, o_ref, n, base):
        g = g_ref[...]

        def body(c, carry):
            sl = _chunk(c)
            q, k, v = q_ref[0, sl, :], k_ref[0, sl, :], v_ref[0, sl, :]
            s = (_dot_nt(q, k) * mask).astype(BF16)
            qf = q.astype(F32)
            lhs = jnp.concatenate([s, (qf * qdec_f).astype(BF16), (qf * qdec_b).astype(BF16)], axis=1)
            y = _dot(lhs, jnp.concatenate([v, st[base + c]], axis=0))
            o_ref[0, sl, :] = _rms(y, g).astype(o_ref.dtype)
            return carry
        lax.fori_loop(0, n, body, 0, unroll=4 * SCAN_UNROLL)

    zero = jnp.zeros((dk, dv), F32)
    carry = phase_a(kc_ref, vc_ref, nc, 0, (zero, zero))
    phase_a(kl_ref, vl_ref, nl, nc, carry)
    phase_b(qc_ref, kc_ref, vc_ref, oc_ref, nc, 0)
    phase_b(ql_ref, kl_ref, vl_ref, ol_ref, nl, nc)


def _retention(log_gamma, qk_c, pv_c, qk_l, pv_l, g, v_blk, batch):
    tc, tl = qk_c.shape[1], qk_l.shape[1]
    dk = qk_c.shape[2] // (2 * H_R)
    dv = g.shape[0] // H_R
    qs = lambda t: pl.BlockSpec((1, t, dk), lambda b, h: (b, 0, h))
    ks = lambda t: pl.BlockSpec((1, t, dk), lambda b, h: (b, 0, H_R + h))
    vs = lambda t: pl.BlockSpec((1, t, dv), lambda b, h: (b, 0, v_blk + h))
    os_ = lambda t: pl.BlockSpec((1, t, dv), lambda b, h: (b, 0, h))
    return pl.pallas_call(
        _retention_kernel,
        grid=(batch, H_R),
        in_specs=[pl.BlockSpec(memory_space=pltpu.SMEM),
                  qs(tc), ks(tc), vs(tc), qs(tl), ks(tl), vs(tl),
                  pl.BlockSpec((1, dv), lambda b, h: (0, h))],
        out_specs=[os_(tc), os_(tl)],
        out_shape=[jax.ShapeDtypeStruct((batch, tc, H_R * dv), BF16),
                   jax.ShapeDtypeStruct((batch, tl, H_R * dv), BF16)],
        scratch_shapes=[pltpu.VMEM(((tc + tl) // CHUNK, 2 * dk, dv), BF16)],
        compiler_params=_params("parallel", "arbitrary"),
        name="retention",
    )(log_gamma, qk_c, qk_c, pv_c, qk_l, qk_l, pv_l, g.reshape(1, -1))


def _lane_scan(x, reverse, op, identity):
    n = x.shape[-1]
    lane = lax.broadcasted_iota(jnp.int32, x.shape, 1)
    s = 1
    while s < n:
        if reverse:
            x = op(x, jnp.where(lane < n - s, pltpu.roll(x, n - s, axis=1), identity))
        else:
            x = op(x, jnp.where(lane >= s, pltpu.roll(x, s, axis=1), identity))
        s *= 2
    return x


_B, _R, _CM, _W, _BE, _ML = range(6)


def _col_of_row(r):
    return jnp.broadcast_to(r, (r.shape[1], r.shape[1])).T


def _mlstm_kernel(gb_ref, qc_ref, kc_ref, vc_ref, gic_ref, gfc_ref, gibc_ref, gfbc_ref,
                  ql_ref, kl_ref, vl_ref, gil_ref, gfl_ref, gibl_ref, gfbl_ref, g_ref,
                  oc_ref, ol_ref, cst, mst, kt, sfc, sbc, sfl, sbl):
    h = pl.program_id(1)
    L = CHUNK
    dk = qc_ref.shape[-1]
    dv = vc_ref.shape[-1]
    assert dk == L
    nc = qc_ref.shape[1] // L
    nl = ql_ref.shape[1] // L
    row = lax.broadcasted_iota(jnp.int32, (L, L), 0)
    col = lax.broadcasted_iota(jnp.int32, (L, L), 1)
    tri = (col <= row, col >= row)
    end_lane = (L - 1, 0)

    ones_col = jnp.ones((L, L), BF16)

    def v_aug(v_ref, c):
        return jnp.concatenate([v_ref[0, _chunk(c), :], ones_col], axis=1)

    def prep(gi_ref, gf_ref, scr, d):
        rev = d == 1
        ig = (gi_ref[0, 0] + gb_ref[2 * d, h]) * LOG2E
        lf = _log_sigmoid(gf_ref[0, 0] + gb_ref[2 * d + 1, h]) * LOG2E
        b = _lane_scan(lf, rev, jnp.add, 0.0)
        r = ig - b
        lane = lax.broadcasted_iota(jnp.int32, b.shape, 1)
        be = jnp.broadcast_to(
            jnp.sum(jnp.where(lane == end_lane[d], b, 0.0), axis=-1, keepdims=True), b.shape)
        loc = be + r
        m_loc = jnp.broadcast_to(jnp.max(loc, axis=-1, keepdims=True), b.shape)
        scr[_B] = b
        scr[_R] = r
        scr[_CM] = _lane_scan(r, rev, jnp.maximum, -jnp.inf)
        scr[_W] = jnp.exp2(loc - m_loc)
        scr[_BE] = be
        scr[_ML] = m_loc

    prep(gic_ref, gfc_ref, sfc, 0)
    prep(gibc_ref, gfbc_ref, sbc, 1)
    prep(gil_ref, gfl_ref, sfl, 0)
    prep(gibl_ref, gfbl_ref, sbl, 1)

    def transpose_k(k_ref, n, base):
        def body(c, carry):
            kt[base + c] = k_ref[0, _chunk(c), :].T
            return carry
        lax.fori_loop(0, n, body, 0, unroll=SCAN_UNROLL)

    transpose_k(kc_ref, nc, 0)
    transpose_k(kl_ref, nl, nc)

    def phase_a(v_ref, scr, n, base, carry):
        def step(d, c, state):
            C, m = state
            s = scr[d]
            cst[d, base + c] = C.astype(BF16)
            mst[d, base + c] = m
            row_of = lambda idx: s[idx, pl.ds(c, 1), :]
            kw = (kt[base + c].astype(F32) * row_of(_W)).astype(BF16)
            dC = _dot(kw, v_aug(v_ref, c))
            be, m_loc = row_of(_BE), row_of(_ML)
            m_new = jnp.maximum(be + m, m_loc)
            f_old = jnp.exp2(be + m - m_new)
            f_new = jnp.exp2(m_loc - m_new)
            wide = lambda f: jnp.concatenate([f] * (C.shape[1] // L), axis=1)
            return wide(f_old) * C + wide(f_new) * dC, m_new

        def body(t, carry):
            return step(0, t, carry[0]), step(1, n - 1 - t, carry[1])
        return lax.fori_loop(0, n, body, carry, unroll=2 * SCAN_UNROLL)

    def phase_b(q_ref, k_ref, v_ref, o_ref, scr, n, base):
        g = g_ref[...]
        ones_dv = jnp.ones((dv, L), BF16)

        def body(c, carry):
            sl = _chunk(c)
            q, k = q_ref[0, sl, :], k_ref[0, sl, :]
            va = v_aug(v_ref, c)
            qk = _dot_nt(q, k)
            qf = q.astype(F32)
            y = None
            for d in (0, 1):
                s = scr[d]
                row_of = lambda idx: s[idx, pl.ds(c, 1), :]
                m_prev = mst[d, base + c]
                m_t = row_of(_B) + jnp.maximum(row_of(_CM), m_prev)
                U = _col_of_row(row_of(_B) - m_t)
                NM = _col_of_row(-m_t)
                p = jnp.exp2(jnp.where(tri[d], U + row_of(_R), -jnp.inf))
                a = jnp.exp2(U + m_prev)
                lhs = jnp.concatenate([(qk * p).astype(BF16), (a * qf).astype(BF16)], axis=1)
                rhs = jnp.concatenate([va, cst[d, base + c]], axis=0)
                res = _dot(lhs, rhs)
                inv = 1.0 / jnp.maximum(jnp.abs(res[:, dv:]), jnp.exp2(NM))
                hval = res[:, :dv] * jnp.concatenate([inv] * (dv // L), axis=1)
                y = hval if y is None else y + hval
            ms = _dot((y * y).astype(BF16), ones_dv) * (1.0 / dv)
            scale = lax.rsqrt(ms + EPS)
            o_ref[0, sl, :] = (y * jnp.concatenate([scale] * (dv // L), axis=1) * g).astype(o_ref.dtype)
            return carry
        lax.fori_loop(0, n, body, 0, unroll=8 * SCAN_UNROLL)

    zero = (jnp.zeros((dk, dv + L), F32), jnp.zeros((1, L), F32))
    scr_c = (sfc, sbc)
    scr_l = (sfl, sbl)
    carry = phase_a(vc_ref, scr_c, nc, 0, (zero, zero))
    phase_a(vl_ref, scr_l, nl, nc, carry)
    phase_b(qc_ref, kc_ref, vc_ref, oc_ref, scr_c, nc, 0)
    phase_b(ql_ref, kl_ref, vl_ref, ol_ref, scr_l, nl, nc)


def _mlstm(gate_b, q_c, k_c, pv_c, gates_c, q_l, k_l, pv_l, gates_l, g, v_blk, batch):
    tc, tl = q_c.shape[1], q_l.shape[1]
    dk = q_c.shape[2] // H_M
    dv = g.shape[0] // H_M
    qs = lambda t: pl.BlockSpec((1, t, dk), lambda b, h: (b, 0, h))
    ks = qs
    vs = lambda t: pl.BlockSpec((1, t, dv), lambda b, h: (b, 0, v_blk + h))
    gs = lambda t, k: pl.BlockSpec((1, 1, t // CHUNK, CHUNK), lambda b, h: (b, k * H_M + h, 0, 0))
    os_ = lambda t: pl.BlockSpec((1, t, dv), lambda b, h: (b, 0, h))

    def stream(t):
        return [qs(t), ks(t), vs(t), gs(t, 0), gs(t, 1), gs(t, 2), gs(t, 3)]

    n_slots = (tc + tl) // CHUNK
    rows = lambda t: pltpu.VMEM((6, t // CHUNK, CHUNK), F32)
    return pl.pallas_call(
        _mlstm_kernel,
        grid=(batch, H_M),
        in_specs=[pl.BlockSpec(memory_space=pltpu.SMEM)] + stream(tc) + stream(tl)
                 + [pl.BlockSpec((1, dv), lambda b, h: (0, h))],
        out_specs=[os_(tc), os_(tl)],
        out_shape=[jax.ShapeDtypeStruct((batch, tc, H_M * dv), BF16),
                   jax.ShapeDtypeStruct((batch, tl, H_M * dv), BF16)],
        scratch_shapes=[pltpu.VMEM((2, n_slots, dk, dv + CHUNK), BF16),
                        pltpu.VMEM((2, n_slots, 1, CHUNK), F32),
                        pltpu.VMEM((n_slots, dk, CHUNK), BF16),
                        rows(tc), rows(tc), rows(tl), rows(tl)],
        compiler_params=_params("parallel", "arbitrary"),
        name="mlstm",
    )(gate_b, q_c, k_c, pv_c, gates_c, gates_c, gates_c, gates_c,
      q_l, k_l, pv_l, gates_l, gates_l, gates_l, gates_l, g.reshape(1, -1))


def _merge_kernel(yr_ref, rg_ref, ym_ref, mo_ref, gr_ref, gm_ref, wr_ref, wm_ref, o_ref):
    a = (yr_ref[...] * _silu(rg_ref[...].astype(F32)).astype(BF16))
    b = (ym_ref[...] * _sigmoid(mo_ref[...].astype(F32)).astype(BF16))
    y = (_sigmoid(gr_ref[...].astype(F32)) * _dot(a, wr_ref[...])
         + _sigmoid(gm_ref[...].astype(F32)) * _dot(b, wm_ref[...]))
    o_ref[...] = y.astype(o_ref.dtype)


def _merge(yr, ym, pa, pb, wr, wm, layer):
    m, vw = yr.shape
    d = wr.shape[2]
    bm = _tile(m, BM_ROW, 16)
    col = lambda k: pl.BlockSpec((bm, vw), lambda i: (i, k))
    wspec = pl.BlockSpec((None, vw, d), lambda i: (layer, 0, 0), pipeline_mode=pl.Buffered(1))
    return pl.pallas_call(
        _merge_kernel,
        grid=(m // bm,),
        in_specs=[col(0), col(1), col(0), col(3), col(0), col(1), wspec, wspec],
        out_specs=pl.BlockSpec((bm, d), lambda i: (i, 0)),
        out_shape=jax.ShapeDtypeStruct((m, d), BF16),
        compiler_params=_params("parallel"),
        name="merge",
    )(yr, pa, ym, pa, pb, pb, wr, wm)


def _resid_kernel(a_ref, w_ref, x_ref, go_ref, gate_ref, gn_ref, sh_ref, sc_ref, xo_ref, ho_ref):
    for r, rb in _row_parts(a_ref.shape[0], BM_ROW):
        rs = slice(r, r + rb)
        out = _dot(a_ref[rs, :], w_ref[...])
        xn = x_ref[rs, :] + gate_ref[0] * _rms(out, go_ref[...])
        xo_ref[rs, :] = xn
        ho_ref[rs, :] = _normmod(xn, gn_ref[...], sh_ref[0], sc_ref[0]).astype(ho_ref.dtype)


def _resid_last_kernel(a_ref, w_ref, x_ref, go_ref, gate_ref, xo_ref):
    for r, rb in _row_parts(a_ref.shape[0], BM_ROW):
        rs = slice(r, r + rb)
        out = _dot(a_ref[rs, :], w_ref[...])
        xo_ref[rs, :] = x_ref[rs, :] + gate_ref[0] * _rms(out, go_ref[...])


def _resid(a, w, layer, x, g_out, gate, rows_per_mod, nxt=None, name="resid"):
    m, k = a.shape
    d = w.shape[2]
    bm = _tile(rows_per_mod, 2 * BM_ROW, 16)
    tiles = 2 * bm * (2 * k + 4 * d + 4 * d + 2 * d) + bm * 4 * d
    if 2 * k * d + tiles > V7X_VMEM_LIMIT:
        bm = _tile(rows_per_mod, BM_ROW, 16)
    per = rows_per_mod // bm
    row = lambda width: pl.BlockSpec((bm, width), lambda i: (i, 0))
    vec = pl.BlockSpec((1, 1, d), lambda i: (i // per, 0, 0))
    gvec = pl.BlockSpec((1, d), lambda i: (0, 0))
    in_specs = [row(k),
                pl.BlockSpec((None, k, d), lambda i: (layer, 0, 0), pipeline_mode=pl.Buffered(1)),
                row(d), gvec, vec]
    args = [a, w, x, g_out.reshape(1, d), gate]
    if nxt is None:
        return pl.pallas_call(
            _resid_last_kernel, grid=(m // bm,), in_specs=in_specs, out_specs=row(d),
            out_shape=jax.ShapeDtypeStruct((m, d), F32),
            compiler_params=_params("parallel"), name=name,
        )(*args)
    g_next, shift, scale = nxt
    return pl.pallas_call(
        _resid_kernel, grid=(m // bm,),
        in_specs=in_specs + [gvec, vec, vec],
        out_specs=[row(d), row(d)],
        out_shape=[jax.ShapeDtypeStruct((m, d), F32), jax.ShapeDtypeStruct((m, d), BF16)],
        compiler_params=_params("parallel"), name=name,
    )(*args, g_next.reshape(1, d), shift, scale)


def _rope_tables(seq, dk, q_scale):
    quarter = dk // 4
    t = jnp.arange(seq)
    rows = (t // GRID_W).astype(F32)
    cols = (t % GRID_W).astype(F32)
    freqs = ROPE_BASE ** (-jnp.arange(quarter, dtype=F32) / quarter)
    ar = rows[:, None] * freqs[None, :]
    ac = cols[:, None] * freqs[None, :]
    cos = jnp.concatenate([jnp.cos(ar), jnp.cos(ar), jnp.cos(ac), jnp.cos(ac)], axis=-1)
    sin = jnp.concatenate([-jnp.sin(ar), jnp.sin(ar), -jnp.sin(ac), jnp.sin(ac)], axis=-1)
    scale = jnp.array([q_scale, 1.0], F32)[:, None, None]
    return cos[None] * scale, sin[None] * scale


def _gates_t_kernel(wt_ref, h_ref, o_ref):
    o_ref[0] = _dot_nt(wt_ref[...], h_ref[...])


def _gates_t(h, wt, layer, batch, seq, name):
    m, d = h.shape
    ng = wt.shape[1]
    bm = _tile(seq, BM_PROJ, 128)
    per = seq // bm
    out = pl.pallas_call(
        _gates_t_kernel,
        grid=(m // bm,),
        in_specs=[pl.BlockSpec((None, ng, d), lambda i: (layer, 0, 0)),
                  pl.BlockSpec((bm, d), lambda i: (i, 0))],
        out_specs=pl.BlockSpec((1, ng, bm), lambda i: (i // per, 0, i % per)),
        out_shape=jax.ShapeDtypeStruct((batch, ng, seq), F32),
        compiler_params=_params("parallel"),
        name=name,
    )(wt, h)
    return out.reshape(batch, ng, seq // CHUNK, CHUNK)


def kernel(x, c, ctx, c_ctx, w_ada, b_ada, norm_w, w_in, mlstm_conv_w, mlstm_conv_b, mlstm_gate_b,
           ret_decay_exp, head_norm_w, w_ret_out, w_mlstm_out, w_o, w_up, ffn_conv_w, ffn_conv_b,
           w_down):
    batch, seq, d = x.shape
    ctx_len = ctx.shape[1]
    depth = w_ada.shape[0]
    vw = w_ret_out.shape[1]
    r_qk = (w_in.shape[2] - 4 * vw - 4 * H_M - 2 * d) // 4
    r_dk = r_qk // H_R
    m_dk = r_qk // H_M
    assert d == vw, "column-block addressing of the projection output assumes D_MODEL == VW"

    o_rv = 2 * r_qk
    o_mq = o_rv + 2 * vw
    o_mv = o_mq + 2 * r_qk
    o_mg = o_mv + 2 * vw
    o_merge = o_mg + 4 * H_M

    pad = (-(batch + 1)) % 8
    cvec = jnp.concatenate([c, c_ctx[None], jnp.zeros((pad, d), F32)], axis=0)
    mod = _ada(cvec, w_ada, b_ada)

    def mod_l(l, k):
        return mod[l, :batch, k * d:(k + 1) * d].reshape(batch, 1, d)

    def mod_c(l, k):
        return mod[l, batch:batch + 1, k * d:(k + 1) * d].reshape(1, 1, d)

    rope = _rope_tables(seq, r_dk, r_dk ** -0.5)
    qscale_r = jnp.concatenate([jnp.full((r_qk,), r_dk ** -0.5, F32), jnp.ones((r_qk,), F32)])
    log_gamma = jnp.log1p(-jnp.exp2(-ret_decay_exp.astype(F32)))

    ml, mc = batch * seq, batch * ctx_len
    xl = x.reshape(ml, d)
    xc = ctx.reshape(mc, d)
    hl = _normmod_call(xl, norm_w[0, 0], mod_l(0, 0), mod_l(0, 1), seq)
    hc = _normmod_call(xc, norm_w[0, 0], mod_c(0, 0), mod_c(0, 1), mc)

    w_in_b = w_in[:, :, :o_mg].astype(BF16)
    w_mg_b = w_in[:, :, o_merge:].astype(BF16)
    w_gate_t = jnp.transpose(w_in[:, :, o_mg:o_merge], (0, 2, 1)).astype(BF16)
    wr_b, wm_b, wo_b, wu_b, wd_b = (w.astype(BF16) for w in (w_ret_out, w_mlstm_out, w_o, w_up, w_down))
    d_ff = w_down.shape[1]

    for l in range(depth):
        last = l == depth - 1
        cw, cb = mlstm_conv_w[l], mlstm_conv_b[l]

        def project(h, seq_len, rope_tabs, nm):
            pa = _proj(h, w_in_b, l, ((o_rv, 2 * vw), (o_mv, 2 * vw)), name=nm + "_plain")
            pb = _proj(h, w_mg_b, l, ((0, 2 * d),), name=nm + "_mgate")
            if rope_tabs is None:
                rqk = _proj(h, w_in_b, l, ((0, o_rv),), colscale=qscale_r, name=nm + "_rqk")
            else:
                rqk = _proj(h, w_in_b, l, ((0, o_rv),), rope=rope_tabs, seq=seq_len, name=nm + "_rqk")
            mq, mk = _convproj(h, w_in_b, l, o_mq, r_qk, cw, cb, seq_len, scale_a=m_dk ** -0.5,
                               name=nm + "_mqk")
            b = h.shape[0] // seq_len
            return (pa, pb, rqk.reshape(b, seq_len, -1),
                    (mq.reshape(b, seq_len, -1), mk.reshape(b, seq_len, -1)),
                    _gates_t(h, w_gate_t, l, b, seq_len, nm + "_gates"))

        pa_c, pb_c, rqk_c, mqk_c, g_c = project(hc, ctx_len, None, "ctx")
        pa_l, pb_l, rqk_l, mqk_l, g_l = project(hl, seq, rope, "lat")
        pv_c = pa_c.reshape(batch, ctx_len, -1)
        pv_l = pa_l.reshape(batch, seq, -1)

        yr_c, yr_l = _retention(log_gamma[l], rqk_c, pv_c, rqk_l, pv_l, head_norm_w[l, 0],
                                0 * H_R, batch)
        ym_c, ym_l = _mlstm(mlstm_gate_b[l], *mqk_c, pv_c, g_c, *mqk_l, pv_l, g_l, head_norm_w[l, 1],
                            2 * H_M, batch)

        def tail(yr, ym, pa, pb, xres, mod_k, rows_per_mod, seq_len, nm):
            y = _merge(yr.reshape(-1, vw), ym.reshape(-1, vw), pa, pb, wr_b, wm_b, l)
            x1, h2 = _resid(y, wo_b, l, xres, norm_w[l, 1], mod_k(l, 2), rows_per_mod,
                            nxt=(norm_w[l, 2], mod_k(l, 3), mod_k(l, 4)), name=nm + "_o")
            act = _convproj(h2, wu_b, l, 0, d_ff, ffn_conv_w[l], ffn_conv_b[l], seq_len, name=nm + "_ffn")
            if last:
                return _resid(act, wd_b, l, x1, norm_w[l, 3], mod_k(l, 5), rows_per_mod,
                              name=nm + "_down"), None
            return _resid(act, wd_b, l, x1, norm_w[l, 3], mod_k(l, 5), rows_per_mod,
                          nxt=(norm_w[l + 1, 0], mod_k(l + 1, 0), mod_k(l + 1, 1)), name=nm + "_down")

        xl, hl = tail(yr_l, ym_l, pa_l, pb_l, xl, mod_l, seq, seq, "lat")
        if not last:
            xc, hc = tail(yr_c, ym_c, pa_c, pb_c, xc, mod_c, mc, ctx_len, "ctx")

    return xl.reshape(batch, seq, d)
```

```python
import functools

import jax
import jax.numpy as jnp
from jax import lax
from jax.experimental import pallas as pl
from jax.experimental.pallas import tpu as pltpu

F32 = jnp.float32
BF16 = jnp.bfloat16

GRID_W = 64
CHUNK = 128
H_R = 8
H_M = 8
N_MOD = 6
CONV_W = 3
ROPE_BASE = 10000.0
EPS = 1e-6
LOG2E = 1.4426950408889634
HALO = 16
ROWS_PART = 512
V7X_VMEM_LIMIT = 56 * 1024 * 1024

BM_PROJ = 1024
BN_PROJ = 2048
BM_CONV = 1024
BN_CONV = 512
BM_ROW = 256
BM_NORM = 512
BN_ADA = 1024
SCAN_UNROLL = 4


def _tile(n, pref, quantum):
    if n <= pref:
        return n
    t = (pref // quantum) * quantum
    while t > quantum and n % t:
        t -= quantum
    assert n % t == 0, (n, pref, quantum)
    return t


def _params(*sem):
    return pltpu.CompilerParams(dimension_semantics=sem, vmem_limit_bytes=V7X_VMEM_LIMIT)


def _sigmoid(x):
    return 1.0 / (1.0 + jnp.exp(-x))


def _silu(x):
    return x * _sigmoid(x)


def _log_sigmoid(x):
    return jnp.minimum(x, 0.0) - jnp.log1p(jnp.exp(-jnp.abs(x)))


def _rms(x, g):
    ms = jnp.mean(x * x, axis=-1, keepdims=True)
    return x * lax.rsqrt(ms + EPS) * g


def _normmod(x, g, shift, scale):
    return _rms(x, g) * (1.0 + scale) + shift


def _dot(a, b):
    return jnp.dot(a, b, preferred_element_type=F32)


def _dot_nt(a, b):
    return lax.dot_general(a, b, (((1,), (1,)), ((), ())), preferred_element_type=F32)


def _dot_tn(a, b):
    return lax.dot_general(a, b, (((0,), (0,)), ((), ())), preferred_element_type=F32)


def _ada_kernel(c_ref, w_ref, b_ref, o_ref):
    s = _silu(c_ref[...]).astype(BF16)
    o_ref[0] = _dot(s, w_ref[0].astype(BF16)) + b_ref[0]


def _ada(cvec, w_ada, b_ada):
    depth, d, n = w_ada.shape
    r = cvec.shape[0]
    bn = _tile(n, BN_ADA, 128)
    return pl.pallas_call(
        _ada_kernel,
        grid=(depth, n // bn),
        in_specs=[pl.BlockSpec((r, d), lambda l, j: (0, 0)),
                  pl.BlockSpec((1, d, bn), lambda l, j: (l, 0, j)),
                  pl.BlockSpec((1, 1, bn), lambda l, j: (l, 0, j))],
        out_specs=pl.BlockSpec((1, r, bn), lambda l, j: (l, 0, j)),
        out_shape=jax.ShapeDtypeStruct((depth, r, n), F32),
        compiler_params=_params("arbitrary", "arbitrary"),
        name="ada",
    )(cvec, w_ada, b_ada.reshape(depth, 1, n))


def _normmod_kernel(x_ref, g_ref, sh_ref, sc_ref, o_ref):
    o_ref[...] = _normmod(x_ref[...], g_ref[...], sh_ref[0], sc_ref[0]).astype(o_ref.dtype)


def _normmod_call(x, g, shift, scale, rows_per_mod):
    m, d = x.shape
    bm = _tile(rows_per_mod, BM_NORM, 16)
    per = rows_per_mod // bm
    vec = pl.BlockSpec((1, 1, d), lambda i: (i // per, 0, 0))
    return pl.pallas_call(
        _normmod_kernel,
        grid=(m // bm,),
        in_specs=[pl.BlockSpec((bm, d), lambda i: (i, 0)),
                  pl.BlockSpec((1, d), lambda i: (0, 0)), vec, vec],
        out_specs=pl.BlockSpec((bm, d), lambda i: (i, 0)),
        out_shape=jax.ShapeDtypeStruct((m, d), BF16),
        compiler_params=_params("parallel"),
        name="normmod",
    )(x, g.reshape(1, d), shift, scale)


def _proj_kernel(h_ref, w_ref, o_ref):
    o_ref[...] = _dot(h_ref[...], w_ref[...]).astype(o_ref.dtype)


def _proj_scale_kernel(h_ref, w_ref, s_ref, o_ref):
    o_ref[...] = (_dot(h_ref[...], w_ref[...]) * s_ref[...]).astype(o_ref.dtype)


def _swap32(r):
    lane = lax.broadcasted_iota(jnp.int32, r.shape, 1)
    return jnp.where((lane & 32) == 0, pltpu.roll(r, 96, axis=1), pltpu.roll(r, 32, axis=1))


def _row_parts(bm, rows):
    rb = rows if bm % rows == 0 else bm
    return [(r, rb) for r in range(0, bm, rb)]


def _proj_rope_kernel(h_ref, w_ref, c_ref, s_ref, o_ref):
    for r0, rb in _row_parts(h_ref.shape[0], ROWS_PART):
        rs = slice(r0, r0 + rb)
        res = _dot(h_ref[rs, :], w_ref[...])
        cos, sin = c_ref[0, rs, :], s_ref[0, rs, :]
        for g in range(res.shape[1] // 128):
            r = res[:, g * 128:(g + 1) * 128]
            o_ref[rs, g * 128:(g + 1) * 128] = (r * cos + _swap32(r) * sin).astype(o_ref.dtype)


def _col_block_map(ranges, bn):
    assert all(first % bn == 0 and cnt % bn == 0 for first, cnt in ranges), (ranges, bn)

    def block(j):
        blk = j + ranges[0][0] // bn
        seen = 0
        for (first, cnt), (nxt, _) in zip(ranges[:-1], ranges[1:]):
            seen += cnt // bn
            blk = blk + jnp.where(j >= seen, (nxt - first - cnt) // bn, 0)
        return blk
    return block


def _proj(h, w, layer, ranges, colscale=None, rope=None, seq=None, name="proj"):
    m, d = h.shape
    n = sum(cnt for _, cnt in ranges)
    bm = _tile(m if seq is None else seq, BM_PROJ, 16)
    bn = _tile(n, BN_PROJ, 128) if rope is None else n // 2
    block = _col_block_map(ranges, bn)
    in_specs = [pl.BlockSpec((bm, d), lambda i, j: (i, 0)),
                pl.BlockSpec((None, d, bn), lambda i, j: (layer, 0, block(j)))]
    args = [h, w]
    kern = _proj_kernel
    if colscale is not None:
        kern = _proj_scale_kernel
        in_specs.append(pl.BlockSpec((1, bn), lambda i, j: (0, j)))
        args.append(colscale.reshape(1, n))
    if rope is not None:
        kern = _proj_rope_kernel
        cos, sin = rope
        assert cos.shape == (n // bn, seq, 128)
        per = seq // bm
        tab = pl.BlockSpec((1, bm, 128), lambda i, j: (j, i % per, 0))
        in_specs += [tab, tab]
        args += [cos, sin]
    return pl.pallas_call(
        kern,
        grid=(m // bm, n // bn),
        in_specs=in_specs,
        out_specs=pl.BlockSpec((bm, bn), lambda i, j: (i, j)),
        out_shape=jax.ShapeDtypeStruct((m, n), BF16),
        compiler_params=_params("parallel", "arbitrary"),
        name=name,
    )(*args)


def _fill_halo(hp_ref, hc_ref, hn_ref, hext, first, last, bm):
    hext[0:HALO, :] = hp_ref[...]
    hext[HALO:HALO + bm, :] = hc_ref[...]
    hext[HALO + bm:, :] = hn_ref[...]

    @pl.when(first)
    def _():
        hext[0:HALO, :] = jnp.zeros((HALO, hext.shape[1]), hext.dtype)

    @pl.when(last)
    def _():
        hext[HALO + bm:, :] = jnp.zeros((HALO, hext.shape[1]), hext.dtype)


def _conv3(u, cw, cb, bm):
    rows = u.shape[0]
    prev = pltpu.roll(u, 1, axis=0)[HALO:HALO + bm]
    cur = u[HALO:HALO + bm]
    nxt = pltpu.roll(u, rows - 1, axis=0)[HALO:HALO + bm]
    return cb + prev * cw[0:1] + cur * cw[1:2] + nxt * cw[2:3]


def _convproj_pair_kernel(hp_ref, hc_ref, hn_ref, wa_ref, wg_ref, cwa_ref, cba_ref, cwg_ref, cbg_ref,
                          oa_ref, og_ref, hext, *, bm, per, scale_a):
    i, j = pl.program_id(0), pl.program_id(1)

    @pl.when(j == 0)
    def _():
        _fill_halo(hp_ref, hc_ref, hn_ref, hext, i % per == 0, i % per == per - 1, bm)

    hx = hext[...]
    a = _conv3(_dot(hx, wa_ref[...]), cwa_ref[...], cba_ref[...], bm)
    oa_ref[...] = (_silu(a) * scale_a).astype(oa_ref.dtype)
    g = _conv3(_dot(hx, wg_ref[...]), cwg_ref[...], cbg_ref[...], bm)
    og_ref[...] = _silu(g).astype(og_ref.dtype)


def _convproj_glu_kernel(hp_ref, hc_ref, hn_ref, wa_ref, wg_ref, cwa_ref, cba_ref, cwg_ref, cbg_ref,
                         o_ref, hext, *, bm, per):
    i, j = pl.program_id(0), pl.program_id(1)

    @pl.when(j == 0)
    def _():
        _fill_halo(hp_ref, hc_ref, hn_ref, hext, i % per == 0, i % per == per - 1, bm)

    hx = hext[...]
    a = _conv3(_dot(hx, wa_ref[...]), cwa_ref[...], cba_ref[...], bm)
    g = _conv3(_dot(hx, wg_ref[...]), cwg_ref[...], cbg_ref[...], bm)
    o_ref[...] = (_silu(a) * g).astype(o_ref.dtype)


def _convproj(h, w, layer, col0, n, cw, cb, seq, scale_a=None, name="convproj"):
    glu = scale_a is None
    m, d = h.shape
    bm = _tile(seq, BM_CONV, HALO)
    bn = _tile(n, BN_CONV, 128)
    assert col0 % bn == 0
    first = col0 // bn
    per = seq // bm
    nb = bm // HALO
    last_blk = m // HALO - 1
    hspecs = [pl.BlockSpec((HALO, d), lambda i, j: (jnp.maximum(i * nb - 1, 0), 0)),
              pl.BlockSpec((bm, d), lambda i, j: (i, 0)),
              pl.BlockSpec((HALO, d), lambda i, j: (jnp.minimum((i + 1) * nb, last_blk), 0))]
    wspec = lambda off: pl.BlockSpec((None, d, bn), lambda i, j: (layer, 0, first + j + off))
    cwspec = lambda off: pl.BlockSpec((CONV_W, bn), lambda i, j: (0, j + off))
    cbspec = lambda off: pl.BlockSpec((1, bn), lambda i, j: (0, j + off))
    cb2 = cb.reshape(1, -1)
    off = n // bn
    if glu:
        kern = functools.partial(_convproj_glu_kernel, bm=bm, per=per)
    else:
        kern = functools.partial(_convproj_pair_kernel, bm=bm, per=per, scale_a=scale_a)
    out_spec = pl.BlockSpec((bm, bn), lambda i, j: (i, j))
    out_shape = jax.ShapeDtypeStruct((m, n), BF16)
    return pl.pallas_call(
        kern,
        grid=(m // bm, n // bn),
        in_specs=hspecs + [wspec(0), wspec(off), cwspec(0), cbspec(0), cwspec(off), cbspec(off)],
        out_specs=out_spec if glu else [out_spec, out_spec],
        out_shape=out_shape if glu else [out_shape, out_shape],
        scratch_shapes=[pltpu.VMEM((bm + 2 * HALO, d), BF16)],
        compiler_params=_params("parallel", "arbitrary"),
        name=name,
    )(h, h, h, w, w, cw, cb2, cw, cb2)


def _chunk(c):
    return pl.ds(pl.multiple_of(c * CHUNK, CHUNK), CHUNK)


def _retention_kernel(lg_ref, qc_ref, kc_ref, vc_ref, ql_ref, kl_ref, vl_ref, g_ref,
                      oc_ref, ol_ref, st):
    h = pl.program_id(1)
    L = CHUNK
    dk = qc_ref.shape[-1]
    dv = vc_ref.shape[-1]
    assert dk == L
    nc = qc_ref.shape[1] // L
    nl = ql_ref.shape[1] // L
    lgf, lgb = lg_ref[0, h], lg_ref[1, h]
    row = lax.broadcasted_iota(jnp.int32, (L, L), 0).astype(F32)
    col = lax.broadcasted_iota(jnp.int32, (L, L), 1).astype(F32)
    rel = row - col
    mask = (jnp.where(rel >= 0, jnp.exp(jnp.maximum(rel, 0.0) * lgf), 0.0)
            + jnp.where(rel <= 0, jnp.exp(jnp.maximum(-rel, 0.0) * lgb), 0.0))
    qdec_f = jnp.exp((row + 1.0) * lgf)
    kdec_f = jnp.exp((L - 1.0 - row) * lgf)
    qdec_b = jnp.exp((L - row) * lgb)
    kdec_b = jnp.exp(row * lgb)
    cdec_f = jnp.exp(jnp.full((1, dv), L, F32) * lgf)
    cdec_b = jnp.exp(jnp.full((1, dv), L, F32) * lgb)

    def phase_a(k_ref, v_ref, n, base, carry):
        def body(t, carry):
            Sf, Sb = carry
            cf, cb = t, n - 1 - t
            st[base + cf, 0:dk, :] = Sf.astype(BF16)
            st[base + cb, dk:, :] = Sb.astype(BF16)
            kf = (k_ref[0, _chunk(cf), :].astype(F32) * kdec_f).astype(BF16)
            kb = (k_ref[0, _chunk(cb), :].astype(F32) * kdec_b).astype(BF16)
            Sf = cdec_f * Sf + _dot_tn(kf, v_ref[0, _chunk(cf), :])
            Sb = cdec_b * Sb + _dot_tn(kb, v_ref[0, _chunk(cb), :])
            return Sf, Sb
        return lax.fori_loop(0, n, body, carry, unroll=4 * SCAN_UNROLL)

    def phase_b(q_ref, k_ref, v_ref, o_ref, n, base):
        g = g_ref[...]

        def body(c, carry):
            sl = _chunk(c)
            q, k, v = q_ref[0, sl, :], k_ref[0, sl, :], v_ref[0, sl, :]
            s = (_dot_nt(q, k) * mask).astype(BF16)
            qf = q.astype(F32)
            lhs = jnp.concatenate([s, (qf * qdec_f).astype(BF16), (qf * qdec_b).astype(BF16)], axis=1)
            y = _dot(lhs, jnp.concatenate([v, st[base + c]], axis=0))
            o_ref[0, sl, :] = _rms(y, g).astype(o_ref.dtype)
            return carry
        lax.fori_loop(0, n, body, 0, unroll=4 * SCAN_UNROLL)

    zero = jnp.zeros((dk, dv), F32)
    carry = phase_a(kc_ref, vc_ref, nc, 0, (zero, zero))
    phase_a(kl_ref, vl_ref, nl, nc, carry)
    phase_b(qc_ref, kc_ref, vc_ref, oc_ref, nc, 0)
    phase_b(ql_ref, kl_ref, vl_ref, ol_ref, nl, nc)


def _retention(log_gamma, qk_c, pv_c, qk_l, pv_l, g, v_blk, batch):
    tc, tl = qk_c.shape[1], qk_l.shape[1]
    dk = qk_c.shape[2] // (2 * H_R)
    dv = g.shape[0] // H_R
    qs = lambda t: pl.BlockSpec((1, t, dk), lambda b, h: (b, 0, h))
    ks = lambda t: pl.BlockSpec((1, t, dk), lambda b, h: (b, 0, H_R + h))
    vs = lambda t: pl.BlockSpec((1, t, dv), lambda b, h: (b, 0, v_blk + h))
    os_ = lambda t: pl.BlockSpec((1, t, dv), lambda b, h: (b, 0, h))
    return pl.pallas_call(
        _retention_kernel,
        grid=(batch, H_R),
        in_specs=[pl.BlockSpec(memory_space=pltpu.SMEM),
                  qs(tc), ks(tc), vs(tc), qs(tl), ks(tl), vs(tl),
                  pl.BlockSpec((1, dv), lambda b, h: (0, h))],
        out_specs=[os_(tc), os_(tl)],
        out_shape=[jax.ShapeDtypeStruct((batch, tc, H_R * dv), BF16),
                   jax.ShapeDtypeStruct((batch, tl, H_R * dv), BF16)],
        scratch_shapes=[pltpu.VMEM(((tc + tl) // CHUNK, 2 * dk, dv), BF16)],
        compiler_params=_params("parallel", "arbitrary"),
        name="retention",
    )(log_gamma, qk_c, qk_c, pv_c, qk_l, qk_l, pv_l, g.reshape(1, -1))


def _lane_scan(x, reverse, op, identity):
    n = x.shape[-1]
    lane = lax.broadcasted_iota(jnp.int32, x.shape, 1)
    s = 1
    while s < n:
        if reverse:
            x = op(x, jnp.where(lane < n - s, pltpu.roll(x, n - s, axis=1), identity))
        else:
            x = op(x, jnp.where(lane >= s, pltpu.roll(x, s, axis=1), identity))
        s *= 2
    return x


_B, _R, _CM, _W, _BE, _ML = range(6)


def _col_of_row(r):
    return jnp.broadcast_to(r, (r.shape[1], r.shape[1])).T


def _mlstm_kernel(gb_ref, qc_ref, kc_ref, vc_ref, gic_ref, gfc_ref, gibc_ref, gfbc_ref,
                  ql_ref, kl_ref, vl_ref, gil_ref, gfl_ref, gibl_ref, gfbl_ref, g_ref,
                  oc_ref, ol_ref, cst, mst, kt, sfc, sbc, sfl, sbl):
    h = pl.program_id(1)
    L = CHUNK
    dk = qc_ref.shape[-1]
    dv = vc_ref.shape[-1]
    assert dk == L
    nc = qc_ref.shape[1] // L
    nl = ql_ref.shape[1] // L
    row = lax.broadcasted_iota(jnp.int32, (L, L), 0)
    col = lax.broadcasted_iota(jnp.int32, (L, L), 1)
    tri = (col <= row, col >= row)
    end_lane = (L - 1, 0)

    ones_col = jnp.ones((L, L), BF16)

    def v_aug(v_ref, c):
        return jnp.concatenate([v_ref[0, _chunk(c), :], ones_col], axis=1)

    def prep(gi_ref, gf_ref, scr, d):
        rev = d == 1
        ig = (gi_ref[0, 0] + gb_ref[2 * d, h]) * LOG2E
        lf = _log_sigmoid(gf_ref[0, 0] + gb_ref[2 * d + 1, h]) * LOG2E
        b = _lane_scan(lf, rev, jnp.add, 0.0)
        r = ig - b
        lane = lax.broadcasted_iota(jnp.int32, b.shape, 1)
        be = jnp.broadcast_to(
            jnp.sum(jnp.where(lane == end_lane[d], b, 0.0), axis=-1, keepdims=True), b.shape)
        loc = be + r
        m_loc = jnp.broadcast_to(jnp.max(loc, axis=-1, keepdims=True), b.shape)
        scr[_B] = b
        scr[_R] = r
        scr[_CM] = _lane_scan(r, rev, jnp.maximum, -jnp.inf)
        scr[_W] = jnp.exp2(loc - m_loc)
        scr[_BE] = be
        scr[_ML] = m_loc

    prep(gic_ref, gfc_ref, sfc, 0)
    prep(gibc_ref, gfbc_ref, sbc, 1)
    prep(gil_ref, gfl_ref, sfl, 0)
    prep(gibl_ref, gfbl_ref, sbl, 1)

    def transpose_k(k_ref, n, base):
        def body(c, carry):
            kt[base + c] = k_ref[0, _chunk(c), :].T
            return carry
        lax.fori_loop(0, n, body, 0, unroll=SCAN_UNROLL)

    transpose_k(kc_ref, nc, 0)
    transpose_k(kl_ref, nl, nc)

    def phase_a(v_ref, scr, n, base, carry):
        def step(d, c, state):
            C, m = state
            s = scr[d]
            cst[d, base + c] = C.astype(BF16)
            mst[d, base + c] = m
            row_of = lambda idx: s[idx, pl.ds(c, 1), :]
            kw = (kt[base + c].astype(F32) * row_of(_W)).astype(BF16)
            dC = _dot(kw, v_aug(v_ref, c))
            be, m_loc = row_of(_BE), row_of(_ML)
            m_new = jnp.maximum(be + m, m_loc)
            f_old = jnp.exp2(be + m - m_new)
            f_new = jnp.exp2(m_loc - m_new)
            wide = lambda f: jnp.concatenate([f] * (C.shape[1] // L), axis=1)
            return wide(f_old) * C + wide(f_new) * dC, m_new

        def body(t, carry):
            return step(0, t, carry[0]), step(1, n - 1 - t, carry[1])
        return lax.fori_loop(0, n, body, carry, unroll=4 * SCAN_UNROLL)

    def phase_b(q_ref, k_ref, v_ref, o_ref, scr, n, base):
        g = g_ref[...]
        ones_dv = jnp.ones((dv, L), BF16)

        def body(c, carry):
            sl = _chunk(c)
            q, k = q_ref[0, sl, :], k_ref[0, sl, :]
            va = v_aug(v_ref, c)
            qk = _dot_nt(q, k)
            qf = q.astype(F32)
            y = None
            for d in (0, 1):
                s = scr[d]
                row_of = lambda idx: s[idx, pl.ds(c, 1), :]
                m_prev = mst[d, base + c]
                m_t = row_of(_B) + jnp.maximum(row_of(_CM), m_prev)
                U = _col_of_row(row_of(_B) - m_t)
                NM = _col_of_row(-m_t)
                p = jnp.exp2(jnp.where(tri[d], U + row_of(_R), -jnp.inf))
                a = jnp.exp2(U + m_prev)
                lhs = jnp.concatenate([(qk * p).astype(BF16), (a * qf).astype(BF16)], axis=1)
                rhs = jnp.concatenate([va, cst[d, base + c]], axis=0)
                res = _dot(lhs, rhs)
                inv = 1.0 / jnp.maximum(jnp.abs(res[:, dv:]), jnp.exp2(NM))
                hval = res[:, :dv] * jnp.concatenate([inv] * (dv // L), axis=1)
                y = hval if y is None else y + hval
            ms = _dot((y * y).astype(BF16), ones_dv) * (1.0 / dv)
            scale = lax.rsqrt(ms + EPS)
            o_ref[0, sl, :] = (y * jnp.concatenate([scale] * (dv // L), axis=1) * g).astype(o_ref.dtype)
            return carry
        lax.fori_loop(0, n, body, 0, unroll=8 * SCAN_UNROLL)

    zero = (jnp.zeros((dk, dv + L), F32), jnp.zeros((1, L), F32))
    scr_c = (sfc, sbc)
    scr_l = (sfl, sbl)
    carry = phase_a(vc_ref, scr_c, nc, 0, (zero, zero))
    phase_a(vl_ref, scr_l, nl, nc, carry)
    phase_b(qc_ref, kc_ref, vc_ref, oc_ref, scr_c, nc, 0)
    phase_b(ql_ref, kl_ref, vl_ref, ol_ref, scr_l, nl, nc)


def _mlstm(gate_b, q_c, k_c, pv_c, gates_c, q_l, k_l, pv_l, gates_l, g, v_blk, batch):
    tc, tl = q_c.shape[1], q_l.shape[1]
    dk = q_c.shape[2] // H_M
    dv = g.shape[0] // H_M
    qs = lambda t: pl.BlockSpec((1, t, dk), lambda b, h: (b, 0, h))
    ks = qs
    vs = lambda t: pl.BlockSpec((1, t, dv), lambda b, h: (b, 0, v_blk + h))
    gs = lambda t, k: pl.BlockSpec((1, 1, t // CHUNK, CHUNK), lambda b, h: (b, k * H_M + h, 0, 0))
    os_ = lambda t: pl.BlockSpec((1, t, dv), lambda b, h: (b, 0, h))

    def stream(t):
        return [qs(t), ks(t), vs(t), gs(t, 0), gs(t, 1), gs(t, 2), gs(t, 3)]

    n_slots = (tc + tl) // CHUNK
    rows = lambda t: pltpu.VMEM((6, t // CHUNK, CHUNK), F32)
    return pl.pallas_call(
        _mlstm_kernel,
        grid=(batch, H_M),
        in_specs=[pl.BlockSpec(memory_space=pltpu.SMEM)] + stream(tc) + stream(tl)
                 + [pl.BlockSpec((1, dv), lambda b, h: (0, h))],
        out_specs=[os_(tc), os_(tl)],
        out_shape=[jax.ShapeDtypeStruct((batch, tc, H_M * dv), BF16),
                   jax.ShapeDtypeStruct((batch, tl, H_M * dv), BF16)],
        scratch_shapes=[pltpu.VMEM((2, n_slots, dk, dv + CHUNK), BF16),
                        pltpu.VMEM((2, n_slots, 1, CHUNK), F32),
                        pltpu.VMEM((n_slots, dk, CHUNK), BF16),
                        rows(tc), rows(tc), rows(tl), rows(tl)],
        compiler_params=_params("parallel", "arbitrary"),
        name="mlstm",
    )(gate_b, q_c, k_c, pv_c, gates_c, gates_c, gates_c, gates_c,
      q_l, k_l, pv_l, gates_l, gates_l, gates_l, gates_l, g.reshape(1, -1))


def _merge_kernel(yr_ref, rg_ref, ym_ref, mo_ref, gr_ref, gm_ref, wr_ref, wm_ref, o_ref):
    a = (yr_ref[...] * _silu(rg_ref[...].astype(F32)).astype(BF16))
    b = (ym_ref[...] * _sigmoid(mo_ref[...].astype(F32)).astype(BF16))
    y = (_sigmoid(gr_ref[...].astype(F32)) * _dot(a, wr_ref[...])
         + _sigmoid(gm_ref[...].astype(F32)) * _dot(b, wm_ref[...]))
    o_ref[...] = y.astype(o_ref.dtype)


def _merge(yr, ym, pa, pb, wr, wm, layer):
    m, vw = yr.shape
    d = wr.shape[2]
    bm = _tile(m, BM_ROW, 16)
    col = lambda k: pl.BlockSpec((bm, vw), lambda i: (i, k))
    wspec = pl.BlockSpec((None, vw, d), lambda i: (layer, 0, 0), pipeline_mode=pl.Buffered(1))
    return pl.pallas_call(
        _merge_kernel,
        grid=(m // bm,),
        in_specs=[col(0), col(1), col(0), col(3), col(0), col(1), wspec, wspec],
        out_specs=pl.BlockSpec((bm, d), lambda i: (i, 0)),
        out_shape=jax.ShapeDtypeStruct((m, d), BF16),
        compiler_params=_params("parallel"),
        name="merge",
    )(yr, pa, ym, pa, pb, pb, wr, wm)


def _resid_kernel(a_ref, w_ref, x_ref, go_ref, gate_ref, gn_ref, sh_ref, sc_ref, xo_ref, ho_ref):
    for r, rb in _row_parts(a_ref.shape[0], BM_ROW):
        rs = slice(r, r + rb)
        out = _dot(a_ref[rs, :], w_ref[...])
        xn = x_ref[rs, :] + gate_ref[0] * _rms(out, go_ref[...])
        xo_ref[rs, :] = xn
        ho_ref[rs, :] = _normmod(xn, gn_ref[...], sh_ref[0], sc_ref[0]).astype(ho_ref.dtype)


def _resid_last_kernel(a_ref, w_ref, x_ref, go_ref, gate_ref, xo_ref):
    for r, rb in _row_parts(a_ref.shape[0], BM_ROW):
        rs = slice(r, r + rb)
        out = _dot(a_ref[rs, :], w_ref[...])
        xo_ref[rs, :] = x_ref[rs, :] + gate_ref[0] * _rms(out, go_ref[...])


def _resid(a, w, layer, x, g_out, gate, rows_per_mod, nxt=None, name="resid"):
    m, k = a.shape
    d = w.shape[2]
    bm = _tile(rows_per_mod, 2 * BM_ROW, 16)
    tiles = 2 * bm * (2 * k + 4 * d + 4 * d + 2 * d) + bm * 4 * d
    if 2 * k * d + tiles > V7X_VMEM_LIMIT:
        bm = _tile(rows_per_mod, BM_ROW, 16)
    per = rows_per_mod // bm
    row = lambda width: pl.BlockSpec((bm, width), lambda i: (i, 0))
    vec = pl.BlockSpec((1, 1, d), lambda i: (i // per, 0, 0))
    gvec = pl.BlockSpec((1, d), lambda i: (0, 0))
    in_specs = [row(k),
                pl.BlockSpec((None, k, d), lambda i: (layer, 0, 0), pipeline_mode=pl.Buffered(1)),
                row(d), gvec, vec]
    args = [a, w, x, g_out.reshape(1, d), gate]
    if nxt is None:
        return pl.pallas_call(
            _resid_last_kernel, grid=(m // bm,), in_specs=in_specs, out_specs=row(d),
            out_shape=jax.ShapeDtypeStruct((m, d), F32),
            compiler_params=_params("parallel"), name=name,
        )(*args)
    g_next, shift, scale = nxt
    return pl.pallas_call(
        _resid_kernel, grid=(m // bm,),
        in_specs=in_specs + [gvec, vec, vec],
        out_specs=[row(d), row(d)],
        out_shape=[jax.ShapeDtypeStruct((m, d), F32), jax.ShapeDtypeStruct((m, d), BF16)],
        compiler_params=_params("parallel"), name=name,
    )(*args, g_next.reshape(1, d), shift, scale)


def _rope_tables(seq, dk, q_scale):
    quarter = dk // 4
    t = jnp.arange(seq)
    rows = (t // GRID_W).astype(F32)
    cols = (t % GRID_W).astype(F32)
    freqs = ROPE_BASE ** (-jnp.arange(quarter, dtype=F32) / quarter)
    ar = rows[:, None] * freqs[None, :]
    ac = cols[:, None] * freqs[None, :]
    cos = jnp.concatenate([jnp.cos(ar), jnp.cos(ar), jnp.cos(ac), jnp.cos(ac)], axis=-1)
    sin = jnp.concatenate([-jnp.sin(ar), jnp.sin(ar), -jnp.sin(ac), jnp.sin(ac)], axis=-1)
    scale = jnp.array([q_scale, 1.0], F32)[:, None, None]
    return cos[None] * scale, sin[None] * scale


def _gates_t_kernel(wt_ref, h_ref, o_ref):
    o_ref[0] = _dot_nt(wt_ref[...], h_ref[...])


def _gates_t(h, wt, layer, batch, seq, name):
    m, d = h.shape
    ng = wt.shape[1]
    bm = _tile(seq, BM_PROJ, 128)
    per = seq // bm
    out = pl.pallas_call(
        _gates_t_kernel,
        grid=(m // bm,),
        in_specs=[pl.BlockSpec((None, ng, d), lambda i: (layer, 0, 0)),
                  pl.BlockSpec((bm, d), lambda i: (i, 0))],
        out_specs=pl.BlockSpec((1, ng, bm), lambda i: (i // per, 0, i % per)),
        out_shape=jax.ShapeDtypeStruct((batch, ng, seq), F32),
        compiler_params=_params("parallel"),
        name=name,
    )(wt, h)
    return out.reshape(batch, ng, seq // CHUNK, CHUNK)


def kernel(x, c, ctx, c_ctx, w_ada, b_ada, norm_w, w_in, mlstm_conv_w, mlstm_conv_b, mlstm_gate_b,
           ret_decay_exp, head_norm_w, w_ret_out, w_mlstm_out, w_o, w_up, ffn_conv_w, ffn_conv_b,
           w_down):
    batch, seq, d = x.shape
    ctx_len = ctx.shape[1]
    depth = w_ada.shape[0]
    vw = w_ret_out.shape[1]
    r_qk = (w_in.shape[2] - 4 * vw - 4 * H_M - 2 * d) // 4
    r_dk = r_qk // H_R
    m_dk = r_qk // H_M
    assert d == vw, "column-block addressing of the projection output assumes D_MODEL == VW"

    o_rv = 2 * r_qk
    o_mq = o_rv + 2 * vw
    o_mv = o_mq + 2 * r_qk
    o_mg = o_mv + 2 * vw
    o_merge = o_mg + 4 * H_M

    pad = (-(batch + 1)) % 8
    cvec = jnp.concatenate([c, c_ctx[None], jnp.zeros((pad, d), F32)], axis=0)
    mod = _ada(cvec, w_ada, b_ada)

    def mod_l(l, k):
        return mod[l, :batch, k * d:(k + 1) * d].reshape(batch, 1, d)

    def mod_c(l, k):
        return mod[l, batch:batch + 1, k * d:(k + 1) * d].reshape(1, 1, d)

    rope = _rope_tables(seq, r_dk, r_dk ** -0.5)
    qscale_r = jnp.concatenate([jnp.full((r_qk,), r_dk ** -0.5, F32), jnp.ones((r_qk,), F32)])
    log_gamma = jnp.log1p(-jnp.exp2(-ret_decay_exp.astype(F32)))

    ml, mc = batch * seq, batch * ctx_len
    xl = x.reshape(ml, d)
    xc = ctx.reshape(mc, d)
    hl = _normmod_call(xl, norm_w[0, 0], mod_l(0, 0), mod_l(0, 1), seq)
    hc = _normmod_call(xc, norm_w[0, 0], mod_c(0, 0), mod_c(0, 1), mc)

    w_in_b = w_in[:, :, :o_mg].astype(BF16)
    w_mg_b = w_in[:, :, o_merge:].astype(BF16)
    w_gate_t = jnp.transpose(w_in[:, :, o_mg:o_merge], (0, 2, 1)).astype(BF16)
    wr_b, wm_b, wo_b, wu_b, wd_b = (w.astype(BF16) for w in (w_ret_out, w_mlstm_out, w_o, w_up, w_down))
    d_ff = w_down.shape[1]

    for l in range(depth):
        last = l == depth - 1
        cw, cb = mlstm_conv_w[l], mlstm_conv_b[l]

        def project(h, seq_len, rope_tabs, nm):
            pa = _proj(h, w_in_b, l, ((o_rv, 2 * vw), (o_mv, 2 * vw)), name=nm + "_plain")
            pb = _proj(h, w_mg_b, l, ((0, 2 * d),), name=nm + "_mgate")
            if rope_tabs is None:
                rqk = _proj(h, w_in_b, l, ((0, o_rv),), colscale=qscale_r, name=nm + "_rqk")
            else:
                rqk = _proj(h, w_in_b, l, ((0, o_rv),), rope=rope_tabs, seq=seq_len, name=nm + "_rqk")
            mq, mk = _convproj(h, w_in_b, l, o_mq, r_qk, cw, cb, seq_len, scale_a=m_dk ** -0.5,
                               name=nm + "_mqk")
            b = h.shape[0] // seq_len
            return (pa, pb, rqk.reshape(b, seq_len, -1),
                    (mq.reshape(b, seq_len, -1), mk.reshape(b, seq_len, -1)),
                    _gates_t(h, w_gate_t, l, b, seq_len, nm + "_gates"))

        pa_c, pb_c, rqk_c, mqk_c, g_c = project(hc, ctx_len, None, "ctx")
        pa_l, pb_l, rqk_l, mqk_l, g_l = project(hl, seq, rope, "lat")
        pv_c = pa_c.reshape(batch, ctx_len, -1)
        pv_l = pa_l.reshape(batch, seq, -1)

        yr_c, yr_l = _retention(log_gamma[l], rqk_c, pv_c, rqk_l, pv_l, head_norm_w[l, 0],
                                0 * H_R, batch)
        ym_c, ym_l = _mlstm(mlstm_gate_b[l], *mqk_c, pv_c, g_c, *mqk_l, pv_l, g_l, head_norm_w[l, 1],
                            2 * H_M, batch)

        def tail(yr, ym, pa, pb, xres, mod_k, rows_per_mod, seq_len, nm):
            y = _merge(yr.reshape(-1, vw), ym.reshape(-1, vw), pa, pb, wr_b, wm_b, l)
            x1, h2 = _resid(y, wo_b, l, xres, norm_w[l, 1], mod_k(l, 2), rows_per_mod,
                            nxt=(norm_w[l, 2], mod_k(l, 3), mod_k(l, 4)), name=nm + "_o")
            act = _convproj(h2, wu_b, l, 0, d_ff, ffn_conv_w[l], ffn_conv_b[l], seq_len, name=nm + "_ffn")
            if last:
                return _resid(act, wd_b, l, x1, norm_w[l, 3], mod_k(l, 5), rows_per_mod,
                              name=nm + "_down"), None
            return _resid(act, wd_b, l, x1, norm_w[l, 3], mod_k(l, 5), rows_per_mod,
                          nxt=(norm_w[l + 1, 0], mod_k(l + 1, 0), mod_k(l + 1, 1)), name=nm + "_down")

        xl, hl = tail(yr_l, ym_l, pa_l, pb_l, xl, mod_l, seq, seq, "lat")
        if not last:
            xc, hc = tail(yr_c, ym_c, pa_c, pb_c, xc, mod_c, mc, ctx_len, "ctx")

    return xl.reshape(batch, seq, d)
```
